```python
import jax, jax.numpy as jnp
from jax import lax
import numpy as np

D_MODEL = 2048
BATCH = 4
SEQ = 2048
DEPTH = 4
DEC_BATCH = 128
DEC_SEQ = 4
PAST_LEN = 16384
PAGE_SIZE = 128

D_MIX = D_MODEL
D_POOL = D_MIX // 4
POOL_WINDOWS = (2, 4, 8, 16)
N_POOL_GROUPS = len(POOL_WINDOWS)
POOL_GROUP = D_POOL // N_POOL_GROUPS
POOL_HIST = max(POOL_WINDOWS) - 1
D_CONV = 3 * D_MIX // 8
CONV_WIDTH = 31
CONV_HIST = CONV_WIDTH - 1
D_CHUNK = D_MIX - D_POOL - D_CONV
CHUNK = 128
CHUNK_HEAD = 128
N_CHUNK_HEADS = D_CHUNK // CHUNK_HEAD
P_IN = D_POOL + 2 * D_CONV + 2 * D_CHUNK
D_FF = 256 * ((8 * D_MODEL // 3 + 255) // 256)
N_EXPERTS = 8
TOP_K = 2
D_FF_EXPERT = 7 * D_MODEL // 2
N_DENSE = (DEPTH + 1) // 2
N_MOE = DEPTH // 2
EPS = 1e-6

kernel_name = "hymba_pool_conv_chunkmlp_moe_decoder_step"


def rms_norm(x, g):
    x32 = x.astype(jnp.float32)
    y = x32 * lax.rsqrt(jnp.mean(x32 * x32, axis=-1, keepdims=True) + EPS)
    return (y * g.astype(jnp.float32)).astype(x.dtype)


def layer_norm(x, g, b):
    x32 = x.astype(jnp.float32)
    mu = jnp.mean(x32, axis=-1, keepdims=True)
    var = jnp.mean(jnp.square(x32 - mu), axis=-1, keepdims=True)
    y = (x32 - mu) * lax.rsqrt(var + EPS) * g.astype(jnp.float32) + b.astype(jnp.float32)
    return y.astype(x.dtype)


def ada_modulation(c, w_ada, b_ada):
    mod = jax.nn.silu(c) @ w_ada + b_ada
    return jnp.split(mod[:, None, :], 6, axis=-1)


def pool_mixer(a, hist, start_pos, w_pool, pool_scale):
    B, T, _ = a.shape
    ext = jnp.concatenate([hist, a], axis=1)
    ext32 = ext.astype(jnp.float32)
    cs = jnp.concatenate([jnp.zeros((B, 1, D_POOL), jnp.float32), jnp.cumsum(ext32, axis=1)], axis=1)
    t = np.arange(T)
    hi = POOL_HIST + t + 1
    outs = []
    for g, w in enumerate(POOL_WINDOWS):
        sl = slice(g * POOL_GROUP, (g + 1) * POOL_GROUP)
        lo = hi - w
        cnt = np.minimum(start_pos + t + 1, w).astype(np.float32)
        csg = cs[:, :, sl]
        mean = (csg[:, hi] - csg[:, lo]) / cnt[None, :, None]
        diff = (mean - ext32[:, POOL_HIST:, sl]).astype(a.dtype)
        outs.append(diff @ w_pool[g])
    out = jnp.concatenate(outs, axis=-1) * pool_scale
    return out, ext[:, -POOL_HIST:]


def conv_mixer(a, gate, hist, w_dw, b_dw, ln_g, ln_b):
    glu = a * jax.nn.sigmoid(gate)
    ext = jnp.concatenate([hist, glu], axis=1)
    y = lax.conv_general_dilated(ext, w_dw[:, None, :], window_strides=(1,), padding='VALID',
                                 dimension_numbers=('NWC', 'WIO', 'NWC'),
                                 feature_group_count=D_CONV) + b_dw
    y = jax.nn.silu(layer_norm(y, ln_g, ln_b))
    return y, ext[:, -CONV_HIST:]


def spatial_gate(u, v, w_s, b_s):
    B, T, _ = u.shape
    L = min(T, CHUNK)
    n = T // L
    mask = np.tril(np.ones((L, L), dtype=bool))
    wm = jnp.where(mask, w_s[:, :L, :L], 0.0)
    vh = v.reshape(B, n, L, N_CHUNK_HEADS, CHUNK_HEAD)
    s = jnp.einsum('hqk,bnkhc->bnqhc', wm, vh) + b_s[:, :L].T[None, None, :, :, None]
    return u * s.reshape(B, T, D_CHUNK)


def token_mixer(h, hist_pool, hist_conv, start_pos, l, prm):
    proj = h @ prm['w_in'][l]
    c1 = D_POOL
    c2 = c1 + D_CONV
    c3 = c2 + D_CONV
    c4 = c3 + D_CHUNK
    a_pool, a_conv, g_conv, z_u, z_v = (proj[..., :c1], proj[..., c1:c2], proj[..., c2:c3],
                                        proj[..., c3:c4], proj[..., c4:])
    o_pool, new_pool = pool_mixer(a_pool, hist_pool, start_pos, prm['w_pool'][l], prm['pool_scale'][l])
    o_conv, new_conv = conv_mixer(a_conv, g_conv, hist_conv, prm['w_dw'][l], prm['b_dw'][l],
                                  prm['conv_ln_g'][l], prm['conv_ln_b'][l])
    u = jax.nn.gelu(z_u)
    v = layer_norm(jax.nn.gelu(z_v), prm['sgu_ln_g'][l], prm['sgu_ln_b'][l])
    o_chunk = spatial_gate(u, v, prm['w_spatial'][l], prm['b_spatial'][l])
    mix = jnp.concatenate([o_pool, o_conv, o_chunk], axis=-1) @ prm['w_out'][l]
    return mix, new_pool, new_conv, v


def swiglu(x, wg, wu, wd):
    return (jax.nn.silu(x @ wg) * (x @ wu)) @ wd


def moe_swiglu(x, w_router, wg, wu, wd):
    B, T, D = x.shape
    xt = x.reshape(B * T, D)
    logits = (xt @ w_router).astype(jnp.float32)
    top_v, top_i = lax.top_k(logits, TOP_K)
    top_w = jax.nn.softmax(top_v, axis=-1)
    gates = jnp.sum(jax.nn.one_hot(top_i, N_EXPERTS, dtype=jnp.float32) * top_w[..., None], axis=1)
    out = jnp.zeros((B * T, D), jnp.float32)
    for e in range(N_EXPERTS):
        out = out + gates[:, e:e + 1] * swiglu(xt, wg[e], wu[e], wd[e]).astype(jnp.float32)
    return out.astype(x.dtype).reshape(B, T, D)


def trunk(x, c, hist_pool, hist_conv, start_pos, prm):
    new_pool, new_conv, new_v = [], [], []
    for l in range(DEPTH):
        sh1, sc1, gt1, sh2, sc2, gt2 = ada_modulation(c, prm['w_ada'][l], prm['b_ada'][l])
        h = rms_norm(x, prm['g_mix'][l]) * (1.0 + sc1) + sh1
        mix, hp, hc, v = token_mixer(h, hist_pool[l], hist_conv[l], start_pos, l, prm)
        x = x + gt1 * mix
        h = rms_norm(x, prm['g_ffn'][l]) * (1.0 + sc2) + sh2
        if l % 2 == 0:
            i = l // 2
            f = swiglu(h, prm['w_ff_gate'][i], prm['w_ff_up'][i], prm['w_ff_down'][i])
        else:
            i = l // 2
            f = moe_swiglu(h, prm['w_router'][i], prm['w_exp_gate'][i], prm['w_exp_up'][i],
                           prm['w_exp_down'][i])
        x = x + gt2 * f
        new_pool.append(hp)
        new_conv.append(hc)
        new_v.append(v)
    y = rms_norm(x, prm['g_final'])
    return y, jnp.stack(new_pool), jnp.stack(new_conv), new_v


def setup_inputs(seed: int = 0) -> dict:
    key = jax.random.key(seed)
    ks = jax.random.split(key, 32)
    f32 = jnp.float32
    nrm = lambda k, shape, s: jax.random.normal(k, shape, f32) * s
    return {
        'x_prompt': nrm(ks[0], (BATCH, SEQ, D_MODEL), 1.0),
        'x_sample': nrm(ks[1], (DEC_BATCH, DEC_SEQ, D_MODEL), 1.0),
        'state_pool': nrm(ks[2], (DEPTH, DEC_BATCH, POOL_HIST, D_POOL), 1.0),
        'state_conv': nrm(ks[3], (DEPTH, DEC_BATCH, CONV_HIST, D_CONV), 0.5),
        'c_prompt': nrm(ks[4], (BATCH, D_MODEL), 1.0),
        'c_sample': nrm(ks[5], (DEC_BATCH, D_MODEL), 1.0),
        'w_ada': nrm(ks[6], (DEPTH, D_MODEL, 6 * D_MODEL), 0.5 * D_MODEL ** -0.5),
        'b_ada': nrm(ks[7], (DEPTH, 6 * D_MODEL), 0.02),
        'g_mix': 1.0 + nrm(ks[8], (DEPTH, D_MODEL), 0.05),
        'g_ffn': 1.0 + nrm(ks[9], (DEPTH, D_MODEL), 0.05),
        'w_in': nrm(ks[10], (DEPTH, D_MODEL, P_IN), D_MODEL ** -0.5),
        'w_pool': nrm(ks[11], (DEPTH, N_POOL_GROUPS, POOL_GROUP, POOL_GROUP), POOL_GROUP ** -0.5),
        'pool_scale': 1.0 + nrm(ks[12], (DEPTH, D_POOL), 0.1),
        'w_dw': nrm(ks[13], (DEPTH, CONV_WIDTH, D_CONV), CONV_WIDTH ** -0.5),
        'b_dw': nrm(ks[14], (DEPTH, D_CONV), 0.02),
        'conv_ln_g': 1.0 + nrm(ks[15], (DEPTH, D_CONV), 0.05),
        'conv_ln_b': nrm(ks[16], (DEPTH, D_CONV), 0.02),
        'sgu_ln_g': 1.0 + nrm(ks[17], (DEPTH, D_CHUNK), 0.05),
        'sgu_ln_b': nrm(ks[18], (DEPTH, D_CHUNK), 0.02),
        'w_spatial': nrm(ks[19], (DEPTH, N_CHUNK_HEADS, CHUNK, CHUNK), CHUNK ** -0.5),
        'b_spatial': 1.0 + nrm(ks[20], (DEPTH, N_CHUNK_HEADS, CHUNK), 0.05),
        'w_out': nrm(ks[21], (DEPTH, D_MIX, D_MODEL), D_MIX ** -0.5),
        'w_ff_gate': nrm(ks[22], (N_DENSE, D_MODEL, D_FF), D_MODEL ** -0.5),
        'w_ff_up': nrm(ks[23], (N_DENSE, D_MODEL, D_FF), D_MODEL ** -0.5),
        'w_ff_down': nrm(ks[24], (N_DENSE, D_FF, D_MODEL), D_FF ** -0.5),
        'w_router': nrm(ks[25], (N_MOE, D_MODEL, N_EXPERTS), D_MODEL ** -0.5),
        'w_exp_gate': nrm(ks[26], (N_MOE, N_EXPERTS, D_MODEL, D_FF_EXPERT), D_MODEL ** -0.5),
        'w_exp_up': nrm(ks[27], (N_MOE, N_EXPERTS, D_MODEL, D_FF_EXPERT), D_MODEL ** -0.5),
        'w_exp_down': nrm(ks[28], (N_MOE, N_EXPERTS, D_FF_EXPERT, D_MODEL), D_FF_EXPERT ** -0.5),
        'g_final': 1.0 + nrm(ks[29], (D_MODEL,), 0.05),
    }


def reference(x_prompt, x_sample, state_pool, state_conv, c_prompt, c_sample, w_ada, b_ada, g_mix, g_ffn,
              w_in, w_pool, pool_scale, w_dw, b_dw, conv_ln_g, conv_ln_b, sgu_ln_g, sgu_ln_b, w_spatial,
              b_spatial, w_out, w_ff_gate, w_ff_up, w_ff_down, w_router, w_exp_gate, w_exp_up, w_exp_down,
              g_final):
    prm = dict(w_ada=w_ada, b_ada=b_ada, g_mix=g_mix, g_ffn=g_ffn, w_in=w_in, w_pool=w_pool,
               pool_scale=pool_scale, w_dw=w_dw, b_dw=b_dw, conv_ln_g=conv_ln_g, conv_ln_b=conv_ln_b,
               sgu_ln_g=sgu_ln_g, sgu_ln_b=sgu_ln_b, w_spatial=w_spatial, b_spatial=b_spatial, w_out=w_out,
               w_ff_gate=w_ff_gate, w_ff_up=w_ff_up, w_ff_down=w_ff_down, w_router=w_router,
               w_exp_gate=w_exp_gate, w_exp_up=w_exp_up, w_exp_down=w_exp_down, g_final=g_final)
    zero_pool = jnp.zeros((DEPTH, x_prompt.shape[0], POOL_HIST, D_POOL), x_prompt.dtype)
    zero_conv = jnp.zeros((DEPTH, x_prompt.shape[0], CONV_HIST, D_CONV), x_prompt.dtype)
    y_prompt, pool_p, conv_p, _ = trunk(x_prompt, c_prompt, zero_pool, zero_conv, 0, prm)
    y_sample, pool_s, conv_s, v_s = trunk(x_sample, c_sample, state_pool, state_conv, PAST_LEN, prm)
    chunk_v_s = jnp.stack(v_s)
    return (y_prompt, y_sample, pool_p, conv_p, pool_s, conv_s, chunk_v_s)
```

```python
import functools

import jax
import jax.numpy as jnp
from jax import lax
from jax.experimental import pallas as pl
from jax.experimental.pallas import tpu as pltpu

f32 = jnp.float32
bf16 = jnp.bfloat16
i32 = jnp.int32

D = 2048
BATCH = 4
SEQ = 2048
DEPTH = 4
DEC_BATCH = 128
DEC_SEQ = 4
PAST_LEN = 16384
D_POOL = 512
POOL_WINDOWS = (2, 4, 8, 16)
POOL_GROUP = 128
POOL_HIST = 15
D_CONV = 768
CONV_WIDTH = 31
CONV_HIST = 30
D_CHUNK = 768
CHUNK = 128
N_HEADS = 6
P_IN = 3584
D_FF = 5632
N_EXPERTS = 8
D_FF_EXPERT = 7168
EPS = 1e-6

N_PROMPT = BATCH * SEQ
N_SAMPLE = DEC_BATCH * DEC_SEQ
N_TOK = N_PROMPT + N_SAMPLE
N_SEQ = BATCH + DEC_BATCH
N_SEQ_PAD = 136

C1 = D_POOL
C2 = C1 + D_CONV
C3 = C2 + D_CONV
C4 = C3 + D_CHUNK

VMEM_LIMIT = 56 * 1024 * 1024
TM = 512
TM_OUT = 256
TN_ADA = 1024
TN_IN = 512
TF_DENSE = 256
TT = 256
ROW_CHUNK = 64
POOL_HALO = 16
CONV_HALO = 32
SB = 32
TM_MOE = 768
TF_MOE = 256
TM_COMB = 256
MAX_TILES = (N_TOK * 2) // TM_MOE + N_EXPERTS
P_ROWS = MAX_TILES * TM_MOE


def _cparams(sem):
    return pltpu.CompilerParams(dimension_semantics=sem, vmem_limit_bytes=VMEM_LIMIT)


def _sigmoid(x):
    return jax.nn.sigmoid(x)


def _rms(x, g):
    return x * lax.rsqrt(jnp.mean(x * x, axis=-1, keepdims=True) + EPS) * g


def _layer_norm(y, g, b):
    mu = jnp.mean(y, axis=-1, keepdims=True)
    d = y - mu
    var = jnp.mean(d * d, axis=-1, keepdims=True)
    return d * lax.rsqrt(var + EPS) * g + b


def _sel_mod(i, tm, p_ref, s_ref):
    seq = jnp.minimum(i // (SEQ // tm), BATCH - 1)
    return jnp.where(i >= N_PROMPT // tm, s_ref[...], p_ref[pl.ds(seq, 1), :])


def _mod_specs(l, chunk, tm, nargs):
    npt = N_PROMPT // tm
    if nargs == 1:
        mp = lambda i: (l, 0, chunk)
        ms = lambda i: (l, jnp.maximum(i - npt, 0), chunk)
    elif nargs == 2:
        mp = lambda i, j: (l, 0, chunk)
        ms = lambda i, j: (l, jnp.maximum(i - npt, 0), chunk)
    else:
        mp = lambda i, s: (l, 0, chunk)
        ms = lambda i, s: (l, jnp.maximum(i - npt, 0), chunk)
    return pl.BlockSpec((None, 8, D), mp), pl.BlockSpec((None, tm, D), ms, pipeline_mode=pl.Buffered(1))


def _ada_kernel(c_ref, w_ref, b_ref, o_ref):
    c = c_ref[...]
    s = (c * _sigmoid(c)).astype(bf16)
    o_ref[...] = jnp.dot(s, w_ref[...].astype(bf16), preferred_element_type=f32) + b_ref[...]


def _ada_call(c_all, w_ada, b_ada):
    nj = 6 * D // TN_ADA
    return pl.pallas_call(
        _ada_kernel,
        grid=(DEPTH, nj),
        in_specs=[
            pl.BlockSpec((N_SEQ_PAD, D), lambda l, j: (0, 0)),
            pl.BlockSpec((None, D, TN_ADA), lambda l, j: (l, 0, j)),
            pl.BlockSpec((None, 1, TN_ADA), lambda l, j: (l, 0, j)),
        ],
        out_specs=pl.BlockSpec((None, N_SEQ_PAD, TN_ADA), lambda l, j: (l, 0, j)),
        out_shape=jax.ShapeDtypeStruct((DEPTH, N_SEQ_PAD, 6 * D), f32),
        compiler_params=_cparams(("arbitrary", "arbitrary")),
        name="ada_mod",
    )(c_all, w_ada, b_ada.reshape(DEPTH, 1, 6 * D))


def _inproj_kernel(x_ref, shp, shs, scp, scs, g_ref, w_ref, o_ref, h_scr):
    i = pl.program_id(0)

    @pl.when(pl.program_id(1) == 0)
    def _():
        y = _rms(x_ref[...], g_ref[...])
        h = y * (1.0 + _sel_mod(i, TM, scp, scs)) + _sel_mod(i, TM, shp, shs)
        h_scr[...] = h.astype(bf16)

    o_ref[...] = jnp.dot(h_scr[...], w_ref[...].astype(bf16), preferred_element_type=f32)


def _inproj_call(l, x, modp, mods, g_mix, w_in):
    shp, shs = _mod_specs(l, 0, TM, 2)
    scp, scs = _mod_specs(l, 1, TM, 2)
    return pl.pallas_call(
        _inproj_kernel,
        grid=(N_TOK // TM, P_IN // TN_IN),
        in_specs=[
            pl.BlockSpec((TM, D), lambda i, j: (i, 0)),
            shp, shs, scp, scs,
            pl.BlockSpec((None, 1, D), lambda i, j: (l, 0, 0)),
            pl.BlockSpec((None, D, TN_IN), lambda i, j: (l, 0, j)),
        ],
        out_specs=pl.BlockSpec((TM, TN_IN), lambda i, j: (i, j)),
        out_shape=jax.ShapeDtypeStruct((N_TOK, P_IN), f32),
        scratch_shapes=[pltpu.VMEM((TM, D), bf16)],
        compiler_params=_cparams(("arbitrary", "arbitrary")),
        name="in_proj",
    )(x, modp, mods, modp, mods, g_mix.reshape(DEPTH, 1, D), w_in)


def _mixp_kernel(p_ref, wpool_ref, pscale_ref, wdw_ref, bdw_ref, clg_ref, clb_ref, slg_ref, slb_ref,
                 ws_ref, bs_ref, o_ref, npool_ref, nconv_ref, pext, cext, ybuf):
    t = pl.program_id(1)

    @pl.when(t == 0)
    def _():
        pext[0:POOL_HALO, :] = jnp.zeros((POOL_HALO, D_POOL), f32)
        cext[0:CONV_HALO, :] = jnp.zeros((CONV_HALO, D_CONV), f32)

    pext[POOL_HALO:POOL_HALO + TT, :] = p_ref[:, 0:C1]
    pos = lax.broadcasted_iota(i32, (TT, POOL_GROUP), 0) + t * TT
    for g, w in enumerate(POOL_WINDOWS):
        sl = slice(g * POOL_GROUP, (g + 1) * POOL_GROUP)
        tok = pext[POOL_HALO:POOL_HALO + TT, sl]
        s = tok
        for j in range(1, w):
            s = s + pext[POOL_HALO - j:POOL_HALO - j + TT, sl]
        cnt = jnp.minimum(pos + 1, w).astype(f32)
        diff = (s / cnt - tok).astype(bf16)
        o = jnp.dot(diff, wpool_ref[g].astype(bf16), preferred_element_type=f32) * pscale_ref[:, sl]
        o_ref[:, sl] = o.astype(bf16)
    tail = pext[TT:TT + POOL_HALO, :]
    npool_ref[...] = tail
    pext[0:POOL_HALO, :] = tail

    cext[CONV_HALO:CONV_HALO + TT, :] = p_ref[:, C1:C2] * _sigmoid(p_ref[:, C2:C3])
    off = CONV_HALO - CONV_HIST
    for c in range(D_CONV // 128):
        cs = slice(c * 128, (c + 1) * 128)
        for r in range(TT // ROW_CHUNK):
            r0 = r * ROW_CHUNK
            acc = jnp.zeros((ROW_CHUNK, 128), f32)
            for k in range(CONV_WIDTH):
                acc = acc + cext[r0 + off + k:r0 + off + k + ROW_CHUNK, cs] * wdw_ref[k:k + 1, cs]
            ybuf[r0:r0 + ROW_CHUNK, cs] = acc + bdw_ref[:, cs]
    yn = _layer_norm(ybuf[...], clg_ref[...], clb_ref[...])
    o_ref[:, C1:C2] = (yn * _sigmoid(yn)).astype(bf16)
    ctail = cext[TT:TT + CONV_HALO, :]
    nconv_ref[...] = ctail
    cext[0:CONV_HALO, :] = ctail

    ybuf[...] = _layer_norm(jax.nn.gelu(p_ref[:, C4:P_IN]), slg_ref[...], slb_ref[...])
    causal = (lax.broadcasted_iota(i32, (CHUNK, CHUNK), 0) >= lax.broadcasted_iota(i32, (CHUNK, CHUNK), 1))
    for h in range(N_HEADS):
        hs = slice(h * 128, (h + 1) * 128)
        wm = jnp.where(causal, ws_ref[h], 0.0).astype(bf16)
        for n in range(TT // CHUNK):
            rs = slice(n * CHUNK, (n + 1) * CHUNK)
            s = jnp.dot(wm, ybuf[rs, hs].astype(bf16), preferred_element_type=f32) + bs_ref[h]
            u = jax.nn.gelu(p_ref[rs, C3 + h * 128:C3 + (h + 1) * 128])
            o_ref[rs, C2 + h * 128:C2 + (h + 1) * 128] = (u * s).astype(bf16)


def _mixp_call(l, proj, w_pool, pool_scale, w_dw, b_dw, clg, clb, slg, slb, w_spatial, bs_bcast):
    nt = SEQ // TT
    c2 = lambda b, t: (l, 0, 0)
    return pl.pallas_call(
        _mixp_kernel,
        grid=(BATCH, nt),
        in_specs=[
            pl.BlockSpec((TT, P_IN), lambda b, t: (b * nt + t, 0)),
            pl.BlockSpec((None, 4, POOL_GROUP, POOL_GROUP), lambda b, t: (l, 0, 0, 0)),
            pl.BlockSpec((None, 1, D_POOL), c2),
            pl.BlockSpec((None, CONV_WIDTH, D_CONV), c2),
            pl.BlockSpec((None, 1, D_CONV), c2),
            pl.BlockSpec((None, 1, D_CONV), c2),
            pl.BlockSpec((None, 1, D_CONV), c2),
            pl.BlockSpec((None, 1, D_CHUNK), c2),
            pl.BlockSpec((None, 1, D_CHUNK), c2),
            pl.BlockSpec((None, N_HEADS, CHUNK, CHUNK), lambda b, t: (l, 0, 0, 0)),
            pl.BlockSpec((None, N_HEADS, CHUNK, CHUNK), lambda b, t: (l, 0, 0, 0)),
        ],
        out_specs=[
            pl.BlockSpec((TT, D), lambda b, t: (b * nt + t, 0)),
            pl.BlockSpec((None, POOL_HALO, D_POOL), lambda b, t: (b, 0, 0)),
            pl.BlockSpec((None, CONV_HALO, D_CONV), lambda b, t: (b, 0, 0)),
        ],
        out_shape=[
            jax.ShapeDtypeStruct((N_PROMPT, D), bf16),
            jax.ShapeDtypeStruct((BATCH, POOL_HALO, D_POOL), f32),
            jax.ShapeDtypeStruct((BATCH, CONV_HALO, D_CONV), f32),
        ],
        scratch_shapes=[
            pltpu.VMEM((TT + POOL_HALO, D_POOL), f32),
            pltpu.VMEM((TT + CONV_HALO, D_CONV), f32),
            pltpu.VMEM((TT, D_CONV), f32),
        ],
        compiler_params=_cparams(("arbitrary", "arbitrary")),
        name="mix_prompt",
    )(proj, w_pool, pool_scale.reshape(DEPTH, 1, D_POOL), w_dw, b_dw.reshape(DEPTH, 1, D_CONV),
      clg.reshape(DEPTH, 1, D_CONV), clb.reshape(DEPTH, 1, D_CONV), slg.reshape(DEPTH, 1, D_CHUNK),
      slb.reshape(DEPTH, 1, D_CHUNK), w_spatial, bs_bcast)


def _mixs_kernel(ws_ref, bs_ref, p_ref, hp_ref, hc_ref, wpool_ref, pscale_ref, wdw_ref, bdw_ref,
                 clg_ref, clb_ref, slg_ref, slb_ref, o_ref, glu_ref, v_ref):
    def pool_row(idx, sl):
        if idx < POOL_HIST:
            return hp_ref[idx, :, sl]
        return p_ref[idx - POOL_HIST, :, sl]

    for g, w in enumerate(POOL_WINDOWS):
        sl = slice(g * POOL_GROUP, (g + 1) * POOL_GROUP)
        diffs = []
        for t in range(DEC_SEQ):
            tok = pool_row(POOL_HIST + t, sl)
            s = tok
            for j in range(1, w):
                s = s + pool_row(POOL_HIST + t - j, sl)
            diffs.append(s * (1.0 / w) - tok)
        d = jnp.concatenate(diffs, axis=0).astype(bf16)
        o = jnp.dot(d, wpool_ref[g].astype(bf16), preferred_element_type=f32) * pscale_ref[:, sl]
        for t in range(DEC_SEQ):
            o_ref[t, :, sl] = o[t * SB:(t + 1) * SB].astype(bf16)

    for t in range(DEC_SEQ):
        glu_ref[t] = p_ref[t, :, C1:C2] * _sigmoid(p_ref[t, :, C2:C3])

    def conv_row(idx, cs):
        if idx < CONV_HIST:
            return hc_ref[idx, :, cs]
        return glu_ref[idx - CONV_HIST, :, cs]

    for t in range(DEC_SEQ):
        cols = []
        for c in range(D_CONV // 128):
            cs = slice(c * 128, (c + 1) * 128)
            acc = jnp.zeros((SB, 128), f32)
            for k in range(CONV_WIDTH):
                acc = acc + conv_row(t + k, cs) * wdw_ref[k:k + 1, cs]
            cols.append(acc + bdw_ref[:, cs])
        yn = _layer_norm(jnp.concatenate(cols, axis=-1), clg_ref[...], clb_ref[...])
        o_ref[t, :, C1:C2] = (yn * _sigmoid(yn)).astype(bf16)

    for t in range(DEC_SEQ):
        v_ref[t] = _layer_norm(jax.nn.gelu(p_ref[t, :, C4:P_IN]), slg_ref[...], slb_ref[...])
    for t in range(DEC_SEQ):
        for h in range(N_HEADS):
            hs = slice(h * 128, (h + 1) * 128)
            s = jnp.full((SB, 128), bs_ref[h * DEC_SEQ + t], f32)
            for k in range(t + 1):
                s = s + ws_ref[(h * DEC_SEQ + t) * DEC_SEQ + k] * v_ref[k, :, hs]
            u = jax.nn.gelu(p_ref[t, :, C3 + h * 128:C3 + (h + 1) * 128])
            o_ref[t, :, C2 + h * 128:C2 + (h + 1) * 128] = (u * s).astype(bf16)


def _mixs_call(l, ws_small, bs_small, proj_t, hp_t, hc_t, w_pool, pool_scale, w_dw, b_dw, clg, clb, slg, slb):
    c2 = lambda s: (l, 0, 0)
    return pl.pallas_call(
        _mixs_kernel,
        grid=(DEC_BATCH // SB,),
        in_specs=[
            pl.BlockSpec(memory_space=pltpu.SMEM),
            pl.BlockSpec(memory_space=pltpu.SMEM),
            pl.BlockSpec((DEC_SEQ, SB, P_IN), lambda s: (0, s, 0)),
            pl.BlockSpec((None, POOL_HIST, SB, D_POOL), lambda s: (l, 0, s, 0)),
            pl.BlockSpec((None, CONV_HIST, SB, D_CONV), lambda s: (l, 0, s, 0)),
            pl.BlockSpec((None, 4, POOL_GROUP, POOL_GROUP), lambda s: (l, 0, 0, 0)),
            pl.BlockSpec((None, 1, D_POOL), c2),
            pl.BlockSpec((None, CONV_WIDTH, D_CONV), c2),
            pl.BlockSpec((None, 1, D_CONV), c2),
            pl.BlockSpec((None, 1, D_CONV), c2),
            pl.BlockSpec((None, 1, D_CONV), c2),
            pl.BlockSpec((None, 1, D_CHUNK), c2),
            pl.BlockSpec((None, 1, D_CHUNK), c2),
        ],
        out_specs=[
            pl.BlockSpec((DEC_SEQ, SB, D), lambda s: (0, s, 0)),
            pl.BlockSpec((DEC_SEQ, SB, D_CONV), lambda s: (0, s, 0)),
            pl.BlockSpec((DEC_SEQ, SB, D_CHUNK), lambda s: (0, s, 0)),
        ],
        out_shape=[
            jax.ShapeDtypeStruct((DEC_SEQ, DEC_BATCH, D), bf16),
            jax.ShapeDtypeStruct((DEC_SEQ, DEC_BATCH, D_CONV), f32),
            jax.ShapeDtypeStruct((DEC_SEQ, DEC_BATCH, D_CHUNK), f32),
        ],
        compiler_params=_cparams(("arbitrary",)),
        name="mix_sample",
    )(ws_small, bs_small, proj_t, hp_t, hc_t, w_pool, pool_scale.reshape(DEPTH, 1, D_POOL), w_dw,
      b_dw.reshape(DEPTH, 1, D_CONV), clg.reshape(DEPTH, 1, D_CONV), clb.reshape(DEPTH, 1, D_CONV),
      slg.reshape(DEPTH, 1, D_CHUNK), slb.reshape(DEPTH, 1, D_CHUNK))


def _outproj_kernel(m_ref, w_ref, x_ref, gtp, gts, shp, shs, scp, scs, g_ref, x1_ref, h2_ref, wbf):
    i = pl.program_id(0)

    @pl.when(i == 0)
    def _():
        def cast_rows(r, carry):
            rs = pl.ds(pl.multiple_of(r * 256, 256), 256)
            wbf[rs, :] = w_ref[rs, :].astype(bf16)
            return carry
        lax.fori_loop(0, D // 256, cast_rows, 0)

    mix = jnp.dot(m_ref[...], wbf[...], preferred_element_type=f32)
    x1 = x_ref[...] + _sel_mod(i, TM_OUT, gtp, gts) * mix
    x1_ref[...] = x1
    h2 = _rms(x1, g_ref[...]) * (1.0 + _sel_mod(i, TM_OUT, scp, scs)) + _sel_mod(i, TM_OUT, shp, shs)
    h2_ref[...] = h2.astype(h2_ref.dtype)


def _outproj_call(l, mixcat, w_out, x, modp, mods, g_ffn, h2_dtype):
    gtp, gts = _mod_specs(l, 2, TM_OUT, 1)
    shp, shs = _mod_specs(l, 3, TM_OUT, 1)
    scp, scs = _mod_specs(l, 4, TM_OUT, 1)
    return pl.pallas_call(
        _outproj_kernel,
        grid=(N_TOK // TM_OUT,),
        in_specs=[
            pl.BlockSpec((TM_OUT, D), lambda i: (i, 0)),
            pl.BlockSpec((None, D, D), lambda i: (l, 0, 0), pipeline_mode=pl.Buffered(1)),
            pl.BlockSpec((TM_OUT, D), lambda i: (i, 0)),
            gtp, gts, shp, shs, scp, scs,
            pl.BlockSpec((None, 1, D), lambda i: (l, 0, 0)),
        ],
        out_specs=[
            pl.BlockSpec((TM_OUT, D), lambda i: (i, 0)),
            pl.BlockSpec((TM_OUT, D), lambda i: (i, 0)),
        ],
        out_shape=[
            jax.ShapeDtypeStruct((N_TOK, D), f32),
            jax.ShapeDtypeStruct((N_TOK, D), h2_dtype),
        ],
        scratch_shapes=[pltpu.VMEM((D, D), bf16)],
        compiler_params=_cparams(("arbitrary",)),
        name="out_proj",
    )(mixcat, w_out, x, modp, mods, modp, mods, modp, mods, g_ffn.reshape(DEPTH, 1, D))


def _ffn_kernel(h_ref, wg_ref, wu_ref, wd_ref, x1_ref, gtp, gts, o_ref):
    i = pl.program_id(0)
    j = pl.program_id(1)
    h = h_ref[...]
    g = jnp.dot(h, wg_ref[...].astype(bf16), preferred_element_type=f32)
    u = jnp.dot(h, wu_ref[...].astype(bf16), preferred_element_type=f32)
    a = (g * _sigmoid(g) * u).astype(bf16)
    part = jnp.dot(a, wd_ref[...].astype(bf16), preferred_element_type=f32)

    @pl.when(j == 0)
    def _():
        o_ref[...] = part

    @pl.when(j > 0)
    def _():
        o_ref[...] += part

    @pl.when(j == pl.num_programs(1) - 1)
    def _():
        o_ref[...] = x1_ref[...] + _sel_mod(i, TM, gtp, gts) * o_ref[...]


def _ffn_call(l, h2, wg, wu, wd, x1, modp, mods):
    li = l // 2
    gtp, gts = _mod_specs(l, 5, TM, 2)
    return pl.pallas_call(
        _ffn_kernel,
        grid=(N_TOK // TM, D_FF // TF_DENSE),
        in_specs=[
            pl.BlockSpec((TM, D), lambda i, j: (i, 0)),
            pl.BlockSpec((None, D, TF_DENSE), lambda i, j: (li, 0, j)),
            pl.BlockSpec((None, D, TF_DENSE), lambda i, j: (li, 0, j)),
            pl.BlockSpec((None, TF_DENSE, D), lambda i, j: (li, j, 0)),
            pl.BlockSpec((TM, D), lambda i, j: (i, 0)),
            gtp, gts,
        ],
        out_specs=pl.BlockSpec((TM, D), lambda i, j: (i, 0)),
        out_shape=jax.ShapeDtypeStruct((N_TOK, D), f32),
        compiler_params=_cparams(("arbitrary", "arbitrary")),
        name="ffn_dense",
    )(h2, wg, wu, wd, x1, modp, mods)


def _router_kernel(h_ref, wr_ref, route_ref, cnt_ref, carry):
    i = pl.program_id(0)

    @pl.when(i == 0)
    def _():
        carry[...] = jnp.zeros((8, 128), f32)

    h = h_ref[...]
    hh = h.astype(bf16)
    hl = (h - hh.astype(f32)).astype(bf16)
    w = wr_ref[...]
    wh = w.astype(bf16)
    wl = (w - wh.astype(f32)).astype(bf16)
    logits = (jnp.dot(hh, wh, preferred_element_type=f32) + jnp.dot(hl, wh, preferred_element_type=f32)
              + jnp.dot(hh, wl, preferred_element_type=f32))
    lane = lax.broadcasted_iota(i32, (TM, 128), 1)
    neg = jnp.float32(-jnp.inf)
    lg = jnp.where(lane < N_EXPERTS, logits, neg)
    m1 = jnp.max(lg, axis=-1, keepdims=True)
    i1 = jnp.min(jnp.where(lg == m1, lane, 128), axis=-1, keepdims=True)
    lg2 = jnp.where(lane == i1, neg, lg)
    m2 = jnp.max(lg2, axis=-1, keepdims=True)
    i2 = jnp.min(jnp.where(lg2 == m2, lane, 128), axis=-1, keepdims=True)
    e2 = jnp.exp(m2 - m1)
    g1 = 1.0 / (1.0 + e2)
    g2 = e2 / (1.0 + e2)

    hit1 = lane == i1
    hit2 = lane == i2
    cnt = (hit1 | hit2).astype(f32)
    before = (lax.broadcasted_iota(i32, (TM, TM), 0) > lax.broadcasted_iota(i32, (TM, TM), 1))
    prefix = jnp.dot(before.astype(bf16), cnt.astype(bf16), preferred_element_type=f32) + carry[0:1, :]
    r1 = jnp.sum(jnp.where(hit1, prefix, 0.0), axis=-1, keepdims=True)
    r2 = jnp.sum(jnp.where(hit2, prefix, 0.0), axis=-1, keepdims=True)
    total = carry[...] + jnp.sum(cnt, axis=0, keepdims=True)
    carry[...] = total
    cnt_ref[...] = total

    out = jnp.where(lane == 0, i1.astype(f32), 0.0)
    out = jnp.where(lane == 1, i2.astype(f32), out)
    out = jnp.where(lane == 2, g1, out)
    out = jnp.where(lane == 3, g2, out)
    out = jnp.where(lane == 4, r1, out)
    out = jnp.where(lane == 5, r2, out)
    route_ref[...] = out


def _router_call(h2, wr_pad):
    return pl.pallas_call(
        _router_kernel,
        grid=(N_TOK // TM,),
        in_specs=[
            pl.BlockSpec((TM, D), lambda i: (i, 0)),
            pl.BlockSpec((D, 128), lambda i: (0, 0)),
        ],
        out_specs=[
            pl.BlockSpec((TM, 128), lambda i: (i, 0)),
            pl.BlockSpec((8, 128), lambda i: (0, 0)),
        ],
        out_shape=[
            jax.ShapeDtypeStruct((N_TOK, 128), f32),
            jax.ShapeDtypeStruct((8, 128), f32),
        ],
        scratch_shapes=[pltpu.VMEM((8, 128), f32)],
        compiler_params=_cparams(("arbitrary",)),
        name="moe_router",
    )(h2, wr_pad)


def _dispatch_kernel(pos_ref, h_ref, xs_in_ref, xs_ref, sem):
    del xs_in_ref
    i = pl.program_id(0)

    def row_copy(r, slot):
        p = pos_ref[2 * (i * TM + r) + slot]
        return pltpu.make_async_copy(h_ref.at[pl.ds(r, 1), :], xs_ref.at[pl.ds(p, 1), :], sem)

    def issue(r, carry):
        row_copy(r, 0).start()
        row_copy(r, 1).start()
        return carry

    lax.fori_loop(0, TM, issue, 0)
    for _ in range(2):
        pltpu.make_async_copy(h_ref, xs_ref.at[pl.ds(0, TM), :], sem).wait()


def _dispatch_call(pos_flat, h2, xs_init):
    return pl.pallas_call(
        _dispatch_kernel,
        grid_spec=pltpu.PrefetchScalarGridSpec(
            num_scalar_prefetch=1,
            grid=(N_TOK // TM,),
            in_specs=[
                pl.BlockSpec((TM, D), lambda i, pos: (i, 0)),
                pl.BlockSpec(memory_space=pl.ANY),
            ],
            out_specs=pl.BlockSpec(memory_space=pl.ANY),
            scratch_shapes=[pltpu.SemaphoreType.DMA(())],
        ),
        out_shape=jax.ShapeDtypeStruct((P_ROWS, D), f32),
        input_output_aliases={2: 0},
        compiler_params=_cparams(("arbitrary",)),
        name="moe_dispatch",
    )(pos_flat, h2, xs_init)


def _moe_kernel(te_ref, nv_ref, xs_ref, wg_ref, wu_ref, wd_ref, y_ref, xbf):
    i = pl.program_id(0)
    j = pl.program_id(1)
    valid = i < nv_ref[0]

    @pl.when(jnp.logical_and(valid, j == 0))
    def _():
        xbf[...] = xs_ref[...].astype(bf16)

    @pl.when(jnp.logical_and(jnp.logical_not(valid), j == 0))
    def _():
        y_ref[...] = jnp.zeros((TM_MOE, D), f32)

    @pl.when(valid)
    def _():
        x = xbf[...]
        g = jnp.dot(x, wg_ref[...].astype(bf16), preferred_element_type=f32)
        u = jnp.dot(x, wu_ref[...].astype(bf16), preferred_element_type=f32)
        a = (g * _sigmoid(g) * u).astype(bf16)
        part = jnp.dot(a, wd_ref[...].astype(bf16), preferred_element_type=f32)

        @pl.when(j == 0)
        def _():
            y_ref[...] = part

        @pl.when(j > 0)
        def _():
            y_ref[...] += part


def _moe_call(lm, tile_expert, n_valid, xs, wg, wu, wd):
    nj = D_FF_EXPERT // TF_MOE

    def jj(i, j, nv):
        return jnp.where(i < nv[0], j, nj - 1)

    def xmap(i, j, te, nv):
        return (jnp.minimum(i, nv[0] - 1), 0)

    return pl.pallas_call(
        _moe_kernel,
        grid_spec=pltpu.PrefetchScalarGridSpec(
            num_scalar_prefetch=2,
            grid=(MAX_TILES, nj),
            in_specs=[
                pl.BlockSpec((TM_MOE, D), xmap),
                pl.BlockSpec((None, None, D, TF_MOE), lambda i, j, te, nv: (lm, te[i], 0, jj(i, j, nv))),
                pl.BlockSpec((None, None, D, TF_MOE), lambda i, j, te, nv: (lm, te[i], 0, jj(i, j, nv))),
                pl.BlockSpec((None, None, TF_MOE, D), lambda i, j, te, nv: (lm, te[i], jj(i, j, nv), 0)),
            ],
            out_specs=pl.BlockSpec((TM_MOE, D), lambda i, j, te, nv: (i, 0)),
            scratch_shapes=[pltpu.VMEM((TM_MOE, D), bf16)],
        ),
        out_shape=jax.ShapeDtypeStruct((P_ROWS, D), f32),
        compiler_params=_cparams(("arbitrary", "arbitrary")),
        name="moe_experts",
    )(tile_expert, n_valid, xs, wg, wu, wd)


def _combine_kernel(pos_ref, y_ref, x1_ref, route_ref, gtp, gts, o_ref, buf0, buf1, sem):
    i = pl.program_id(0)

    def row_copy(r, slot, buf):
        p = pos_ref[2 * (i * TM_COMB + r) + slot]
        return pltpu.make_async_copy(y_ref.at[pl.ds(p, 1), :], buf.at[pl.ds(r, 1), :], sem)

    def issue(r, carry):
        row_copy(r, 0, buf0).start()
        row_copy(r, 1, buf1).start()
        return carry

    lax.fori_loop(0, TM_COMB, issue, 0)
    for buf in (buf0, buf1):
        pltpu.make_async_copy(y_ref.at[pl.ds(0, TM_COMB), :], buf, sem).wait()
    f = route_ref[:, 2:3] * buf0[...] + route_ref[:, 3:4] * buf1[...]
    o_ref[...] = x1_ref[...] + _sel_mod(i, TM_COMB, gtp, gts) * f


def _combine_call(l, pos_flat, y, x1, route, modp, mods):
    gtp, gts = _mod_specs(l, 5, TM_COMB, 3)
    return pl.pallas_call(
        _combine_kernel,
        grid_spec=pltpu.PrefetchScalarGridSpec(
            num_scalar_prefetch=1,
            grid=(N_TOK // TM_COMB,),
            in_specs=[
                pl.BlockSpec(memory_space=pl.ANY),
                pl.BlockSpec((TM_COMB, D), lambda i, pos: (i, 0)),
                pl.BlockSpec((TM_COMB, 128), lambda i, pos: (i, 0)),
                gtp, gts,
            ],
            out_specs=pl.BlockSpec((TM_COMB, D), lambda i, pos: (i, 0)),
            scratch_shapes=[
                pltpu.VMEM((TM_COMB, D), f32),
                pltpu.VMEM((TM_COMB, D), f32),
                pltpu.SemaphoreType.DMA(()),
            ],
        ),
        out_shape=jax.ShapeDtypeStruct((N_TOK, D), f32),
        compiler_params=_cparams(("arbitrary",)),
        name="moe_combine",
    )(pos_flat, y, x1, route, modp, mods)


def _final_kernel(x_ref, g_ref, o_ref):
    o_ref[...] = _rms(x_ref[...], g_ref[...])


def _final_call(x, g_final):
    return pl.pallas_call(
        _final_kernel,
        grid=(N_TOK // TM,),
        in_specs=[pl.BlockSpec((TM, D), lambda i: (i, 0)), pl.BlockSpec((1, D), lambda i: (0, 0))],
        out_specs=pl.BlockSpec((TM, D), lambda i: (i, 0)),
        out_shape=jax.ShapeDtypeStruct((N_TOK, D), f32),
        compiler_params=_cparams(("arbitrary",)),
        name="final_norm",
    )(x, g_final.reshape(1, D))


def _moe_layer(l, h2, x1, w_router, wg, wu, wd, modp, mods):
    lm = l // 2
    wr_pad = jnp.pad(w_router[lm], ((0, 0), (0, 128 - N_EXPERTS)))
    route, cnt = _router_call(h2, wr_pad)
    expert = route[:, 0:2].astype(i32)
    rank = route[:, 4:6].astype(i32)
    counts = cnt[0, :N_EXPERTS].astype(i32)
    tiles = (counts + TM_MOE - 1) // TM_MOE
    tile_end = jnp.cumsum(tiles)
    row_start = (tile_end - tiles) * TM_MOE
    pos_flat = (row_start[expert] + rank).reshape(-1)
    n_valid = tile_end[-1]
    tile_ids = jnp.minimum(jnp.arange(MAX_TILES, dtype=i32), n_valid - 1)
    tile_expert = jnp.sum((tile_ids[:, None] >= tile_end[None, :]).astype(i32), axis=1)
    xs = _dispatch_call(pos_flat, h2, jnp.zeros((P_ROWS, D), f32))
    y = _moe_call(lm, tile_expert, n_valid.reshape(1), xs, wg, wu, wd)
    return _combine_call(l, pos_flat, y, x1, route, modp, mods)


def kernel(x_prompt, x_sample, state_pool, state_conv, c_prompt, c_sample, w_ada, b_ada, g_mix, g_ffn, w_in, w_pool, pool_scale, w_dw, b_dw, conv_ln_g, conv_ln_b, sgu_ln_g, sgu_ln_b, w_spatial, b_spatial, w_out, w_ff_gate, w_ff_up, w_ff_down, w_router, w_exp_gate, w_exp_up, w_exp_down, g_final):
    x = jnp.concatenate([x_prompt.reshape(N_PROMPT, D), x_sample.reshape(N_SAMPLE, D)], axis=0)
    c_all = jnp.concatenate([c_prompt, c_sample, jnp.zeros((N_SEQ_PAD - N_SEQ, D), f32)], axis=0)
    modp = _ada_call(c_all, w_ada, b_ada)
    mods = jnp.repeat(modp[:, BATCH:N_SEQ], DEC_SEQ, axis=1)
    hp_t = state_pool.transpose(0, 2, 1, 3)
    hc_t = state_conv.transpose(0, 2, 1, 3)
    bs_bcast = jnp.broadcast_to(b_spatial[:, :, :, None], (DEPTH, N_HEADS, CHUNK, CHUNK))
    ws_small = w_spatial[:, :, :DEC_SEQ, :DEC_SEQ].reshape(DEPTH, -1)
    bs_small = b_spatial[:, :, :DEC_SEQ].reshape(DEPTH, -1)

    pool_p, conv_p, pool_s, conv_s, chunk_v = [], [], [], [], []
    for l in range(DEPTH):
        proj = _inproj_call(l, x, modp, mods, g_mix, w_in)
        mix_p, npool, nconv = _mixp_call(l, proj, w_pool, pool_scale, w_dw, b_dw, conv_ln_g, conv_ln_b,
                                         sgu_ln_g, sgu_ln_b, w_spatial, bs_bcast)
        proj_s = proj[N_PROMPT:].reshape(DEC_BATCH, DEC_SEQ, P_IN)
        mix_s, glu_s, v_s = _mixs_call(l, ws_small[l], bs_small[l], proj_s.transpose(1, 0, 2), hp_t, hc_t,
                                       w_pool, pool_scale, w_dw, b_dw, conv_ln_g, conv_ln_b, sgu_ln_g, sgu_ln_b)
        mixcat = jnp.concatenate([mix_p, mix_s.transpose(1, 0, 2).reshape(N_SAMPLE, D)], axis=0)
        pool_p.append(npool[:, POOL_HALO - POOL_HIST:])
        conv_p.append(nconv[:, CONV_HALO - CONV_HIST:])
        pool_s.append(jnp.concatenate([state_pool[l][:, DEC_SEQ:], proj_s[:, :, :D_POOL]], axis=1))
        conv_s.append(jnp.concatenate([state_conv[l][:, DEC_SEQ:], glu_s.transpose(1, 0, 2)], axis=1))
        chunk_v.append(v_s.transpose(1, 0, 2))
        if l % 2 == 0:
            x1, h2 = _outproj_call(l, mixcat, w_out, x, modp, mods, g_ffn, bf16)
            x = _ffn_call(l, h2, w_ff_gate, w_ff_up, w_ff_down, x1, modp, mods)
        else:
            x1, h2 = _outproj_call(l, mixcat, w_out, x, modp, mods, g_ffn, f32)
            x = _moe_layer(l, h2, x1, w_router, w_exp_gate, w_exp_up, w_exp_down, modp, mods)
    y = _final_call(x, g_final)
    return (y[:N_PROMPT].reshape(BATCH, SEQ, D), y[N_PROMPT:].reshape(DEC_BATCH, DEC_SEQ, D),
            jnp.stack(pool_p), jnp.stack(conv_p), jnp.stack(pool_s), jnp.stack(conv_s), jnp.stack(chunk_v))
```

```python
import functools

import jax
import jax.numpy as jnp
from jax import lax
from jax.experimental import pallas as pl
from jax.experimental.pallas import tpu as pltpu

f32 = jnp.float32
bf16 = jnp.bfloat16
i32 = jnp.int32

D = 2048
BATCH = 4
SEQ = 2048
DEPTH = 4
DEC_BATCH = 128
DEC_SEQ = 4
PAST_LEN = 16384
D_POOL = 512
POOL_WINDOWS = (2, 4, 8, 16)
POOL_GROUP = 128
POOL_HIST = 15
D_CONV = 768
CONV_WIDTH = 31
CONV_HIST = 30
D_CHUNK = 768
CHUNK = 128
N_HEADS = 6
P_IN = 3584
D_FF = 5632
N_EXPERTS = 8
D_FF_EXPERT = 7168
EPS = 1e-6

N_PROMPT = BATCH * SEQ
N_SAMPLE = DEC_BATCH * DEC_SEQ
N_TOK = N_PROMPT + N_SAMPLE
N_SEQ = BATCH + DEC_BATCH
N_SEQ_PAD = 136

C1 = D_POOL
C2 = C1 + D_CONV
C3 = C2 + D_CONV
C4 = C3 + D_CHUNK

VMEM_LIMIT = 56 * 1024 * 1024
TM = 512
TM_OUT = 256
TN_ADA = 1024
TT = 256
ROW_CHUNK = 64
POOL_HALO = 16
CONV_HALO = 32
SB = 32
TF_FFN = 256
SUB_MOE = 768
NSUB_MOE = 3
SUB_DENSE = N_TOK // 16
NSUB_DENSE = 4
TM_COMB = 256
MAX_TILES = (N_TOK * 2) // SUB_MOE + N_EXPERTS
MAX_GROUPS = (MAX_TILES + (NSUB_MOE - 1) * N_EXPERTS) // NSUB_MOE
P_ROWS = MAX_TILES * SUB_MOE


def _cparams(sem):
    return pltpu.CompilerParams(dimension_semantics=sem, vmem_limit_bytes=VMEM_LIMIT)


def _sigmoid(x):
    return jax.nn.sigmoid(x)


def _rms(x, g):
    return x * lax.rsqrt(jnp.mean(x * x, axis=-1, keepdims=True) + EPS) * g


def _layer_norm(y, g, b):
    mu = jnp.mean(y, axis=-1, keepdims=True)
    d = y - mu
    var = jnp.mean(d * d, axis=-1, keepdims=True)
    return d * lax.rsqrt(var + EPS) * g + b


def _sel_mod(i, tm, p_ref, s_ref):
    seq = jnp.minimum(i // (SEQ // tm), BATCH - 1)
    return jnp.where(i >= N_PROMPT // tm, s_ref[...], p_ref[pl.ds(seq, 1), :])


def _mod_specs(l, chunk, tm, nargs):
    npt = N_PROMPT // tm
    if nargs == 1:
        mp = lambda i: (l, 0, chunk)
        ms = lambda i: (l, jnp.maximum(i - npt, 0), chunk)
    elif nargs == 2:
        mp = lambda i, j: (l, 0, chunk)
        ms = lambda i, j: (l, jnp.maximum(i - npt, 0), chunk)
    else:
        mp = lambda i, s: (l, 0, chunk)
        ms = lambda i, s: (l, jnp.maximum(i - npt, 0), chunk)
    return pl.BlockSpec((None, 8, D), mp), pl.BlockSpec((None, tm, D), ms, pipeline_mode=pl.Buffered(1))


def _ada_kernel(c_ref, w_ref, b_ref, o_ref):
    c = c_ref[...]
    s = (c * _sigmoid(c)).astype(bf16)
    o_ref[...] = jnp.dot(s, w_ref[...].astype(bf16), preferred_element_type=f32) + b_ref[...]


def _ada_call(c_all, w_ada, b_ada):
    nj = 6 * D // TN_ADA
    return pl.pallas_call(
        _ada_kernel,
        grid=(DEPTH, nj),
        in_specs=[
            pl.BlockSpec((N_SEQ_PAD, D), lambda l, j: (0, 0)),
            pl.BlockSpec((None, D, TN_ADA), lambda l, j: (l, 0, j)),
            pl.BlockSpec((None, 1, TN_ADA), lambda l, j: (l, 0, j)),
        ],
        out_specs=pl.BlockSpec((None, N_SEQ_PAD, TN_ADA), lambda l, j: (l, 0, j)),
        out_shape=jax.ShapeDtypeStruct((DEPTH, N_SEQ_PAD, 6 * D), f32),
        compiler_params=_cparams(("arbitrary", "arbitrary")),
        name="ada_mod",
    )(c_all, w_ada, b_ada.reshape(DEPTH, 1, 6 * D))


def _modulated_norm(i, tm, x, g_ref, shp, shs, scp, scs):
    h = _rms(x, g_ref[...]) * (1.0 + _sel_mod(i, tm, scp, scs)) + _sel_mod(i, tm, shp, shs)
    return h.astype(bf16)


def _prenorm_kernel(xp_ref, xs_ref, shp, shs, scp, scs, g_ref, x_ref, h_ref):
    i = pl.program_id(0)
    x = jnp.where(i >= N_PROMPT // TM, xs_ref[...], xp_ref[...])
    x_ref[...] = x
    h_ref[...] = _modulated_norm(i, TM, x, g_ref, shp, shs, scp, scs)


def _prenorm_call(x_prompt, x_sample, modp, mods, g_mix):
    assert N_SAMPLE == TM
    shp, shs = _mod_specs(0, 0, TM, 1)
    scp, scs = _mod_specs(0, 1, TM, 1)
    npt = N_PROMPT // TM
    return pl.pallas_call(
        _prenorm_kernel,
        grid=(N_TOK // TM,),
        in_specs=[
            pl.BlockSpec((TM, D), lambda i: (jnp.minimum(i, npt - 1), 0)),
            pl.BlockSpec((TM, D), lambda i: (0, 0), pipeline_mode=pl.Buffered(1)),
            shp, shs, scp, scs,
            pl.BlockSpec((None, 1, D), lambda i: (0, 0, 0)),
        ],
        out_specs=[pl.BlockSpec((TM, D), lambda i: (i, 0)), pl.BlockSpec((TM, D), lambda i: (i, 0))],
        out_shape=[jax.ShapeDtypeStruct((N_TOK, D), f32), jax.ShapeDtypeStruct((N_TOK, D), bf16)],
        compiler_params=_cparams(("arbitrary",)),
        name="prenorm",
    )(x_prompt.reshape(N_PROMPT, D), x_sample.reshape(N_SAMPLE, D), modp, mods, modp, mods,
      g_mix.reshape(DEPTH, 1, D))


def _inproj_kernel(h_ref, w_ref, o_ref):
    o_ref[...] = jnp.dot(h_ref[...], w_ref[...], preferred_element_type=f32)


def _inproj_call(l, h1, w_in_bf):
    return pl.pallas_call(
        _inproj_kernel,
        grid=(N_TOK // TM,),
        in_specs=[
            pl.BlockSpec((TM, D), lambda i: (i, 0)),
            pl.BlockSpec((None, D, P_IN), lambda i: (l, 0, 0), pipeline_mode=pl.Buffered(1)),
        ],
        out_specs=pl.BlockSpec((TM, P_IN), lambda i: (i, 0)),
        out_shape=jax.ShapeDtypeStruct((N_TOK, P_IN), f32),
        compiler_params=_cparams(("arbitrary",)),
        name="in_proj",
    )(h1, w_in_bf)


def _mixp_kernel(p_ref, ms_ref, wpool_ref, pscale_ref, wdw_ref, bdw_ref, clg_ref, clb_ref, slg_ref, slb_ref,
                 ws_ref, bs_ref, o_ref, *state_and_scratch):
    b = pl.program_id(0)

    @pl.when(b < BATCH)
    def _():
        _mixp_body(p_ref, wpool_ref, pscale_ref, wdw_ref, bdw_ref, clg_ref, clb_ref, slg_ref, slb_ref,
                   ws_ref, bs_ref, o_ref, *state_and_scratch)

    @pl.when(jnp.logical_and(b == BATCH, pl.program_id(1) < N_SAMPLE // TT))
    def _():
        o_ref[...] = ms_ref[...]


def _mixp_body(p_ref, wpool_ref, pscale_ref, wdw_ref, bdw_ref, clg_ref, clb_ref, slg_ref, slb_ref,
               ws_ref, bs_ref, o_ref, npool_ref, nconv_ref, pext, cext, ybuf):
    t = pl.program_id(1)

    @pl.when(t == 0)
    def _():
        pext[0:POOL_HALO, :] = jnp.zeros((POOL_HALO, D_POOL), f32)
        cext[0:CONV_HALO, :] = jnp.zeros((CONV_HALO, D_CONV), f32)

    pext[POOL_HALO:POOL_HALO + TT, :] = p_ref[:, 0:C1]
    pos = lax.broadcasted_iota(i32, (TT, POOL_GROUP), 0) + t * TT
    for g, w in enumerate(POOL_WINDOWS):
        sl = slice(g * POOL_GROUP, (g + 1) * POOL_GROUP)
        tok = pext[POOL_HALO:POOL_HALO + TT, sl]
        s = tok
        for j in range(1, w):
            s = s + pext[POOL_HALO - j:POOL_HALO - j + TT, sl]
        cnt = jnp.minimum(pos + 1, w).astype(f32)
        diff = (s / cnt - tok).astype(bf16)
        o = jnp.dot(diff, wpool_ref[g].astype(bf16), preferred_element_type=f32) * pscale_ref[:, sl]
        o_ref[:, sl] = o.astype(bf16)
    tail = pext[TT:TT + POOL_HALO, :]
    npool_ref[...] = tail
    pext[0:POOL_HALO, :] = tail

    cext[CONV_HALO:CONV_HALO + TT, :] = p_ref[:, C1:C2] * _sigmoid(p_ref[:, C2:C3])
    off = CONV_HALO - CONV_HIST
    for c in range(D_CONV // 128):
        cs = slice(c * 128, (c + 1) * 128)
        for r in range(TT // ROW_CHUNK):
            r0 = r * ROW_CHUNK
            acc = jnp.zeros((ROW_CHUNK, 128), f32)
            for k in range(CONV_WIDTH):
                acc = acc + cext[r0 + off + k:r0 + off + k + ROW_CHUNK, cs] * wdw_ref[k:k + 1, cs]
            ybuf[r0:r0 + ROW_CHUNK, cs] = acc + bdw_ref[:, cs]
    yn = _layer_norm(ybuf[...], clg_ref[...], clb_ref[...])
    o_ref[:, C1:C2] = (yn * _sigmoid(yn)).astype(bf16)
    ctail = cext[TT:TT + CONV_HALO, :]
    nconv_ref[...] = ctail
    cext[0:CONV_HALO, :] = ctail

    ybuf[...] = _layer_norm(jax.nn.gelu(p_ref[:, C4:P_IN]), slg_ref[...], slb_ref[...])
    causal = (lax.broadcasted_iota(i32, (CHUNK, CHUNK), 0) >= lax.broadcasted_iota(i32, (CHUNK, CHUNK), 1))
    for h in range(N_HEADS):
        hs = slice(h * 128, (h + 1) * 128)
        wm = jnp.where(causal, ws_ref[h], 0.0).astype(bf16)
        for n in range(TT // CHUNK):
            rs = slice(n * CHUNK, (n + 1) * CHUNK)
            s = jnp.dot(wm, ybuf[rs, hs].astype(bf16), preferred_element_type=f32) + bs_ref[h]
            u = jax.nn.gelu(p_ref[rs, C3 + h * 128:C3 + (h + 1) * 128])
            o_ref[rs, C2 + h * 128:C2 + (h + 1) * 128] = (u * s).astype(bf16)


def _mixp_call(l, proj, mix_sample, w_pool, pool_scale, w_dw, b_dw, clg, clb, slg, slb, w_spatial, bs_bcast):
    nt = SEQ // TT
    last = N_TOK // TT - 1
    c2 = lambda b, t: (l, 0, 0)
    tile = lambda b, t: (jnp.minimum(b * nt + t, last), 0)
    seq = lambda b, t: (jnp.minimum(b, BATCH - 1), 0, 0)
    return pl.pallas_call(
        _mixp_kernel,
        grid=(BATCH + 1, nt),
        in_specs=[
            pl.BlockSpec((TT, P_IN), tile),
            pl.BlockSpec((TT, D), lambda b, t: (jnp.where(b == BATCH, jnp.minimum(t, N_SAMPLE // TT - 1), 0), 0)),
            pl.BlockSpec((None, 4, POOL_GROUP, POOL_GROUP), lambda b, t: (l, 0, 0, 0)),
            pl.BlockSpec((None, 1, D_POOL), c2),
            pl.BlockSpec((None, CONV_WIDTH, D_CONV), c2),
            pl.BlockSpec((None, 1, D_CONV), c2),
            pl.BlockSpec((None, 1, D_CONV), c2),
            pl.BlockSpec((None, 1, D_CONV), c2),
            pl.BlockSpec((None, 1, D_CHUNK), c2),
            pl.BlockSpec((None, 1, D_CHUNK), c2),
            pl.BlockSpec((None, N_HEADS, CHUNK, CHUNK), lambda b, t: (l, 0, 0, 0)),
            pl.BlockSpec((None, N_HEADS, CHUNK, CHUNK), lambda b, t: (l, 0, 0, 0)),
        ],
        out_specs=[
            pl.BlockSpec((TT, D), tile),
            pl.BlockSpec((None, POOL_HALO, D_POOL), seq),
            pl.BlockSpec((None, CONV_HALO, D_CONV), seq),
        ],
        out_shape=[
            jax.ShapeDtypeStruct((N_TOK, D), bf16),
            jax.ShapeDtypeStruct((BATCH, POOL_HALO, D_POOL), f32),
            jax.ShapeDtypeStruct((BATCH, CONV_HALO, D_CONV), f32),
        ],
        scratch_shapes=[
            pltpu.VMEM((TT + POOL_HALO, D_POOL), f32),
            pltpu.VMEM((TT + CONV_HALO, D_CONV), f32),
            pltpu.VMEM((TT, D_CONV), f32),
        ],
        compiler_params=_cparams(("arbitrary", "arbitrary")),
        name="mix_prompt",
    )(proj, mix_sample, w_pool, pool_scale.reshape(DEPTH, 1, D_POOL), w_dw, b_dw.reshape(DEPTH, 1, D_CONV),
      clg.reshape(DEPTH, 1, D_CONV), clb.reshape(DEPTH, 1, D_CONV), slg.reshape(DEPTH, 1, D_CHUNK),
      slb.reshape(DEPTH, 1, D_CHUNK), w_spatial, bs_bcast)


def _mixs_kernel(ws_ref, bs_ref, p_ref, hp_ref, hc_ref, wpool_ref, pscale_ref, wdw_ref, bdw_ref,
                 clg_ref, clb_ref, slg_ref, slb_ref, o_ref, glu_ref, v_ref):
    def pool_row(idx, sl):
        if idx < POOL_HIST:
            return hp_ref[idx, :, sl]
        return p_ref[idx - POOL_HIST, :, sl]

    for g, w in enumerate(POOL_WINDOWS):
        sl = slice(g * POOL_GROUP, (g + 1) * POOL_GROUP)
        diffs = []
        for t in range(DEC_SEQ):
            tok = pool_row(POOL_HIST + t, sl)
            s = tok
            for j in range(1, w):
                s = s + pool_row(POOL_HIST + t - j, sl)
            diffs.append(s * (1.0 / w) - tok)
        d = jnp.concatenate(diffs, axis=0).astype(bf16)
        o = jnp.dot(d, wpool_ref[g].astype(bf16), preferred_element_type=f32) * pscale_ref[:, sl]
        for t in range(DEC_SEQ):
            o_ref[t, :, sl] = o[t * SB:(t + 1) * SB].astype(bf16)

    for t in range(DEC_SEQ):
        glu_ref[t] = p_ref[t, :, C1:C2] * _sigmoid(p_ref[t, :, C2:C3])

    def conv_row(idx, cs):
        if idx < CONV_HIST:
            return hc_ref[idx, :, cs]
        return glu_ref[idx - CONV_HIST, :, cs]

    for t in range(DEC_SEQ):
        cols = []
        for c in range(D_CONV // 128):
            cs = slice(c * 128, (c + 1) * 128)
            acc = jnp.zeros((SB, 128), f32)
            for k in range(CONV_WIDTH):
                acc = acc + conv_row(t + k, cs) * wdw_ref[k:k + 1, cs]
            cols.append(acc + bdw_ref[:, cs])
        yn = _layer_norm(jnp.concatenate(cols, axis=-1), clg_ref[...], clb_ref[...])
        o_ref[t, :, C1:C2] = (yn * _sigmoid(yn)).astype(bf16)

    for t in range(DEC_SEQ):
        v_ref[t] = _layer_norm(jax.nn.gelu(p_ref[t, :, C4:P_IN]), slg_ref[...], slb_ref[...])
    for t in range(DEC_SEQ):
        for h in range(N_HEADS):
            hs = slice(h * 128, (h + 1) * 128)
            s = jnp.full((SB, 128), bs_ref[h * DEC_SEQ + t], f32)
            for k in range(t + 1):
                s = s + ws_ref[(h * DEC_SEQ + t) * DEC_SEQ + k] * v_ref[k, :, hs]
            u = jax.nn.gelu(p_ref[t, :, C3 + h * 128:C3 + (h + 1) * 128])
            o_ref[t, :, C2 + h * 128:C2 + (h + 1) * 128] = (u * s).astype(bf16)


def _mixs_call(l, ws_small, bs_small, proj_t, hp_t, hc_t, w_pool, pool_scale, w_dw, b_dw, clg, clb, slg, slb):
    c2 = lambda s: (l, 0, 0)
    return pl.pallas_call(
        _mixs_kernel,
        grid=(DEC_BATCH // SB,),
        in_specs=[
            pl.BlockSpec(memory_space=pltpu.SMEM),
            pl.BlockSpec(memory_space=pltpu.SMEM),
            pl.BlockSpec((DEC_SEQ, SB, P_IN), lambda s: (0, s, 0)),
            pl.BlockSpec((None, POOL_HIST, SB, D_POOL), lambda s: (l, 0, s, 0)),
            pl.BlockSpec((None, CONV_HIST, SB, D_CONV), lambda s: (l, 0, s, 0)),
            pl.BlockSpec((None, 4, POOL_GROUP, POOL_GROUP), lambda s: (l, 0, 0, 0)),
            pl.BlockSpec((None, 1, D_POOL), c2),
            pl.BlockSpec((None, CONV_WIDTH, D_CONV), c2),
            pl.BlockSpec((None, 1, D_CONV), c2),
            pl.BlockSpec((None, 1, D_CONV), c2),
            pl.BlockSpec((None, 1, D_CONV), c2),
            pl.BlockSpec((None, 1, D_CHUNK), c2),
            pl.BlockSpec((None, 1, D_CHUNK), c2),
        ],
        out_specs=[
            pl.BlockSpec((DEC_SEQ, SB, D), lambda s: (0, s, 0)),
            pl.BlockSpec((DEC_SEQ, SB, D_CONV), lambda s: (0, s, 0)),
            pl.BlockSpec((DEC_SEQ, SB, D_CHUNK), lambda s: (0, s, 0)),
        ],
        out_shape=[
            jax.ShapeDtypeStruct((DEC_SEQ, DEC_BATCH, D), bf16),
            jax.ShapeDtypeStruct((DEC_SEQ, DEC_BATCH, D_CONV), f32),
            jax.ShapeDtypeStruct((DEC_SEQ, DEC_BATCH, D_CHUNK), f32),
        ],
        compiler_params=_cparams(("arbitrary",)),
        name="mix_sample",
    )(ws_small, bs_small, proj_t, hp_t, hc_t, w_pool, pool_scale.reshape(DEPTH, 1, D_POOL), w_dw,
      b_dw.reshape(DEPTH, 1, D_CONV), clg.reshape(DEPTH, 1, D_CONV), clb.reshape(DEPTH, 1, D_CONV),
      slg.reshape(DEPTH, 1, D_CHUNK), slb.reshape(DEPTH, 1, D_CHUNK))


def _outproj_kernel(m_ref, w_ref, x_ref, gtp, gts, shp, shs, scp, scs, g_ref, x1_ref, h2_ref, wbf):
    i = pl.program_id(0)

    @pl.when(i == 0)
    def _():
        def cast_rows(r, carry):
            rs = pl.ds(pl.multiple_of(r * 256, 256), 256)
            wbf[rs, :] = w_ref[rs, :].astype(bf16)
            return carry
        lax.fori_loop(0, D // 256, cast_rows, 0)

    mix = jnp.dot(m_ref[...], wbf[...], preferred_element_type=f32)
    x1 = x_ref[...] + _sel_mod(i, TM_OUT, gtp, gts) * mix
    x1_ref[...] = x1
    h2 = _rms(x1, g_ref[...]) * (1.0 + _sel_mod(i, TM_OUT, scp, scs)) + _sel_mod(i, TM_OUT, shp, shs)
    h2_ref[...] = h2.astype(h2_ref.dtype)


def _outproj_call(l, mixcat, w_out, x, modp, mods, g_ffn, h2_dtype):
    gtp, gts = _mod_specs(l, 2, TM_OUT, 1)
    shp, shs = _mod_specs(l, 3, TM_OUT, 1)
    scp, scs = _mod_specs(l, 4, TM_OUT, 1)
    return pl.pallas_call(
        _outproj_kernel,
        grid=(N_TOK // TM_OUT,),
        in_specs=[
            pl.BlockSpec((TM_OUT, D), lambda i: (i, 0)),
            pl.BlockSpec((None, D, D), lambda i: (l, 0, 0), pipeline_mode=pl.Buffered(1)),
            pl.BlockSpec((TM_OUT, D), lambda i: (i, 0)),
            gtp, gts, shp, shs, scp, scs,
            pl.BlockSpec((None, 1, D), lambda i: (l, 0, 0)),
        ],
        out_specs=[
            pl.BlockSpec((TM_OUT, D), lambda i: (i, 0)),
            pl.BlockSpec((TM_OUT, D), lambda i: (i, 0)),
        ],
        out_shape=[
            jax.ShapeDtypeStruct((N_TOK, D), f32),
            jax.ShapeDtypeStruct((N_TOK, D), h2_dtype),
        ],
        scratch_shapes=[pltpu.VMEM((D, D), bf16)],
        compiler_params=_cparams(("arbitrary",)),
        name="out_proj",
    )(mixcat, w_out, x, modp, mods, modp, mods, modp, mods, g_ffn.reshape(DEPTH, 1, D))


def _group_ffn_kernel(ge_ref, gn_ref, rs_ref, tail_ref, x_hbm, wg_ref, wu_ref, wd_ref, y_hbm,
                      xbf, yacc, wgs, wus, wds, sem, *, sub, nsub, nj):
    del ge_ref
    g = pl.program_id(0)
    j = pl.program_id(1)
    n = gn_ref[g]

    @pl.when(jnp.logical_and(g == 0, j == 0))
    def _():
        yacc[0] = jnp.zeros((sub, D), f32)

        def tail_copy(c):
            r0 = pl.multiple_of(tail_ref[0] + c * sub, 8)
            return pltpu.make_async_copy(yacc.at[0], y_hbm.at[pl.ds(r0, sub), :], sem.at[1, 0])

        def start(c, carry):
            tail_copy(c).start()
            return carry

        def wait(c, carry):
            tail_copy(c).wait()
            return carry

        lax.fori_loop(0, tail_ref[1], start, 0)
        lax.fori_loop(0, tail_ref[1], wait, 0)

    def x_copy(m):
        r0 = pl.multiple_of(rs_ref[g * nsub + m], 8)
        return pltpu.make_async_copy(x_hbm.at[pl.ds(r0, sub), :], yacc.at[m], sem.at[0, m])

    def y_copy(m):
        r0 = pl.multiple_of(rs_ref[g * nsub + m], 8)
        return pltpu.make_async_copy(yacc.at[m], y_hbm.at[pl.ds(r0, sub), :], sem.at[1, m])

    def for_each_subtile(fn):
        def body(m, carry):
            fn(m)
            return carry
        lax.fori_loop(0, n, body, 0)

    @pl.when(jnp.logical_and(n > 0, j == 0))
    def _():
        for_each_subtile(lambda m: x_copy(m).start())

        def land(m):
            x_copy(m).wait()
            xbf[m] = yacc[m].astype(bf16)
            yacc[m] = jnp.zeros((sub, D), f32)
        for_each_subtile(land)

    @pl.when(n > 0)
    def _():
        wgs[...] = wg_ref[...].astype(bf16)
        wus[...] = wu_ref[...].astype(bf16)
        wds[...] = wd_ref[...].astype(bf16)

        def swiglu_slice(m):
            x = xbf[m]
            gate = jnp.dot(x, wgs[...], preferred_element_type=f32)
            up = jnp.dot(x, wus[...], preferred_element_type=f32)
            act = (gate * _sigmoid(gate) * up).astype(bf16)
            yacc[m] += jnp.dot(act, wds[...], preferred_element_type=f32)
        for_each_subtile(swiglu_slice)

    @pl.when(jnp.logical_and(n > 0, j == nj - 1))
    def _():
        for_each_subtile(lambda m: y_copy(m).start())
        for_each_subtile(lambda m: y_copy(m).wait())


def _group_ffn_call(name, layer, group_expert, group_n, row_start, tail, x, wg, wu, wd, out_rows, sub, nsub,
                    n_groups):
    d_ff = wg.shape[-1]
    nj = d_ff // TF_FFN

    def jmap(g, j, gn):
        return jnp.where(gn[g] > 0, j, nj - 1)

    return pl.pallas_call(
        functools.partial(_group_ffn_kernel, sub=sub, nsub=nsub, nj=nj),
        grid_spec=pltpu.PrefetchScalarGridSpec(
            num_scalar_prefetch=4,
            grid=(n_groups, nj),
            in_specs=[
                pl.BlockSpec(memory_space=pl.ANY),
                pl.BlockSpec((None, None, D, TF_FFN), lambda g, j, ge, gn, rs, tl: (layer, ge[g], 0, jmap(g, j, gn))),
                pl.BlockSpec((None, None, D, TF_FFN), lambda g, j, ge, gn, rs, tl: (layer, ge[g], 0, jmap(g, j, gn))),
                pl.BlockSpec((None, None, TF_FFN, D), lambda g, j, ge, gn, rs, tl: (layer, ge[g], jmap(g, j, gn), 0)),
            ],
            out_specs=pl.BlockSpec(memory_space=pl.ANY),
            scratch_shapes=[
                pltpu.VMEM((nsub, sub, D), bf16),
                pltpu.VMEM((nsub, sub, D), f32),
                pltpu.VMEM((D, TF_FFN), bf16),
                pltpu.VMEM((D, TF_FFN), bf16),
                pltpu.VMEM((TF_FFN, D), bf16),
                pltpu.SemaphoreType.DMA((2, nsub)),
            ],
        ),
        out_shape=jax.ShapeDtypeStruct((out_rows, D), f32),
        compiler_params=_cparams(("arbitrary", "arbitrary")),
        name=name,
    )(group_expert, group_n, row_start, tail, x, wg, wu, wd)


def _dense_ffn_call(l, h2, wg, wu, wd):
    n_groups = N_TOK // (SUB_DENSE * NSUB_DENSE)
    zeros = jnp.zeros((n_groups,), i32)
    return _group_ffn_call(
        "ffn_dense", l // 2, zeros, zeros + NSUB_DENSE, jnp.arange(n_groups * NSUB_DENSE, dtype=i32) * SUB_DENSE,
        jnp.zeros((2,), i32), h2, wg[:, None], wu[:, None], wd[:, None], N_TOK, SUB_DENSE, NSUB_DENSE, n_groups)


def _residual_out(i, tm, x2, ep, outs, final):
    if final:
        (g_ref,), (yp_ref, ys_ref) = ep, outs
        y = _rms(x2, g_ref[...])
        npt = N_PROMPT // tm

        @pl.when(i < npt)
        def _():
            yp_ref[...] = y

        @pl.when(i >= npt)
        def _():
            ys_ref[...] = y
    else:
        (shp, shs, scp, scs, g_ref), (x_ref, h_ref) = ep, outs
        x_ref[...] = x2
        h_ref[...] = _modulated_norm(i, tm, x2, g_ref, shp, shs, scp, scs)


def _residual_specs(l, tm, nargs, g_mix, g_final):
    if nargs == 1:
        tile = lambda i: (i, 0)
        const3 = lambda i: (l + 1, 0, 0)
        const2 = lambda i: (0, 0)
        pmap = lambda i: (jnp.minimum(i, N_PROMPT // tm - 1), 0)
        smap = lambda i: (jnp.maximum(i - N_PROMPT // tm, 0), 0)
    else:
        tile = lambda i, s: (i, 0)
        const3 = lambda i, s: (l + 1, 0, 0)
        const2 = lambda i, s: (0, 0)
        pmap = lambda i, s: (jnp.minimum(i, N_PROMPT // tm - 1), 0)
        smap = lambda i, s: (jnp.maximum(i - N_PROMPT // tm, 0), 0)
    if l == DEPTH - 1:
        return ([g_final.reshape(1, D)], [pl.BlockSpec((1, D), const2)],
                [pl.BlockSpec((tm, D), pmap), pl.BlockSpec((tm, D), smap)],
                [jax.ShapeDtypeStruct((N_PROMPT, D), f32), jax.ShapeDtypeStruct((N_SAMPLE, D), f32)])
    mod_nargs = 1 if nargs == 1 else 3
    return (None, [*_mod_specs(l + 1, 0, tm, mod_nargs), *_mod_specs(l + 1, 1, tm, mod_nargs),
                   pl.BlockSpec((None, 1, D), const3)],
            [pl.BlockSpec((tm, D), tile), pl.BlockSpec((tm, D), tile)],
            [jax.ShapeDtypeStruct((N_TOK, D), f32), jax.ShapeDtypeStruct((N_TOK, D), bf16)])


def _dense_residual_kernel(y_ref, x1_ref, gtp, gts, *rest, final):
    i = pl.program_id(0)
    n_ep = 1 if final else 5
    x2 = x1_ref[...] + _sel_mod(i, TM, gtp, gts) * y_ref[...]
    _residual_out(i, TM, x2, rest[:n_ep], rest[n_ep:], final)


def _dense_residual_call(l, y, x1, modp, mods, g_mix, g_final):
    final = l == DEPTH - 1
    gtp, gts = _mod_specs(l, 5, TM, 1)
    ep_ops, ep_specs, out_specs, out_shape = _residual_specs(l, TM, 1, g_mix, g_final)
    if ep_ops is None:
        ep_ops = [modp, mods, modp, mods, g_mix.reshape(DEPTH, 1, D)]
    return pl.pallas_call(
        functools.partial(_dense_residual_kernel, final=final),
        grid=(N_TOK // TM,),
        in_specs=[pl.BlockSpec((TM, D), lambda i: (i, 0)), pl.BlockSpec((TM, D), lambda i: (i, 0)), gtp, gts, *ep_specs],
        out_specs=out_specs,
        out_shape=out_shape,
        compiler_params=_cparams(("arbitrary",)),
        name="ffn_residual",
    )(y, x1, modp, mods, *ep_ops)


def _router_kernel(h_ref, wr_ref, route_ref, cnt_ref, carry):
    i = pl.program_id(0)

    @pl.when(i == 0)
    def _():
        carry[...] = jnp.zeros((8, 128), f32)

    h = h_ref[...]
    hh = h.astype(bf16)
    hl = (h - hh.astype(f32)).astype(bf16)
    w = wr_ref[...]
    wh = w.astype(bf16)
    wl = (w - wh.astype(f32)).astype(bf16)
    logits = (jnp.dot(hh, wh, preferred_element_type=f32) + jnp.dot(hl, wh, preferred_element_type=f32)
              + jnp.dot(hh, wl, preferred_element_type=f32))
    lane = lax.broadcasted_iota(i32, (TM, 128), 1)
    neg = jnp.float32(-jnp.inf)
    lg = jnp.where(lane < N_EXPERTS, logits, neg)
    m1 = jnp.max(lg, axis=-1, keepdims=True)
    i1 = jnp.min(jnp.where(lg == m1, lane, 128), axis=-1, keepdims=True)
    lg2 = jnp.where(lane == i1, neg, lg)
    m2 = jnp.max(lg2, axis=-1, keepdims=True)
    i2 = jnp.min(jnp.where(lg2 == m2, lane, 128), axis=-1, keepdims=True)
    e2 = jnp.exp(m2 - m1)
    g1 = 1.0 / (1.0 + e2)
    g2 = e2 / (1.0 + e2)

    hit1 = lane == i1
    hit2 = lane == i2
    cnt = (hit1 | hit2).astype(f32)
    before = (lax.broadcasted_iota(i32, (TM, TM), 0) > lax.broadcasted_iota(i32, (TM, TM), 1))
    prefix = jnp.dot(before.astype(bf16), cnt.astype(bf16), preferred_element_type=f32) + carry[0:1, :]
    r1 = jnp.sum(jnp.where(hit1, prefix, 0.0), axis=-1, keepdims=True)
    r2 = jnp.sum(jnp.where(hit2, prefix, 0.0), axis=-1, keepdims=True)
    total = carry[...] + jnp.sum(cnt, axis=0, keepdims=True)
    carry[...] = total
    cnt_ref[...] = total

    out = jnp.where(lane == 0, i1.astype(f32), 0.0)
    out = jnp.where(lane == 1, i2.astype(f32), out)
    out = jnp.where(lane == 2, g1, out)
    out = jnp.where(lane == 3, g2, out)
    out = jnp.where(lane == 4, r1, out)
    out = jnp.where(lane == 5, r2, out)
    route_ref[...] = out


def _router_call(h2, wr_pad):
    return pl.pallas_call(
        _router_kernel,
        grid=(N_TOK // TM,),
        in_specs=[
            pl.BlockSpec((TM, D), lambda i: (i, 0)),
            pl.BlockSpec((D, 128), lambda i: (0, 0)),
        ],
        out_specs=[
            pl.BlockSpec((TM, 128), lambda i: (i, 0)),
            pl.BlockSpec((8, 128), lambda i: (0, 0)),
        ],
        out_shape=[
            jax.ShapeDtypeStruct((N_TOK, 128), f32),
            jax.ShapeDtypeStruct((8, 128), f32),
        ],
        scratch_shapes=[pltpu.VMEM((8, 128), f32)],
        compiler_params=_cparams(("arbitrary",)),
        name="moe_router",
    )(h2, wr_pad)


def _dispatch_kernel(pos_ref, pad_ref, h_ref, xs_ref, zbuf, sem, zsem, tsem):
    i = pl.program_id(0)

    def zero_copy(r):
        return pltpu.make_async_copy(zbuf.at[pl.ds(0, 1), :], xs_ref.at[pl.ds(r, 1), :], zsem)

    def tail_copy(c):
        r0 = pl.multiple_of(pad_ref[2 * N_EXPERTS] + c * SUB_MOE, 8)
        return pltpu.make_async_copy(zbuf, xs_ref.at[pl.ds(r0, SUB_MOE), :], tsem)

    def for_each_pad_row(fn):
        for e in range(N_EXPERTS):
            def body(r, carry):
                fn(r)
                return carry
            lax.fori_loop(pad_ref[e], pad_ref[N_EXPERTS + e], body, 0)

    def for_each_tail_tile(fn):
        def body(c, carry):
            fn(c)
            return carry
        lax.fori_loop(0, pad_ref[2 * N_EXPERTS + 1], body, 0)

    @pl.when(i == 0)
    def _():
        zbuf[...] = jnp.zeros((SUB_MOE, D), f32)
        for_each_pad_row(lambda r: zero_copy(r).start())
        for_each_tail_tile(lambda c: tail_copy(c).start())

    def row_copy(r, slot):
        p = pos_ref[2 * (i * TM + r) + slot]
        return pltpu.make_async_copy(h_ref.at[pl.ds(r, 1), :], xs_ref.at[pl.ds(p, 1), :], sem)

    def issue(r, carry):
        row_copy(r, 0).start()
        row_copy(r, 1).start()
        return carry

    lax.fori_loop(0, TM, issue, 0)
    for _ in range(2):
        pltpu.make_async_copy(h_ref, xs_ref.at[pl.ds(0, TM), :], sem).wait()

    @pl.when(i == 0)
    def _():
        for_each_pad_row(lambda r: zero_copy(r).wait())
        for_each_tail_tile(lambda c: tail_copy(c).wait())


def _dispatch_call(pos_flat, pad_bounds, h2):
    return pl.pallas_call(
        _dispatch_kernel,
        grid_spec=pltpu.PrefetchScalarGridSpec(
            num_scalar_prefetch=2,
            grid=(N_TOK // TM,),
            in_specs=[pl.BlockSpec((TM, D), lambda i, pos, pad: (i, 0))],
            out_specs=pl.BlockSpec(memory_space=pl.ANY),
            scratch_shapes=[pltpu.VMEM((SUB_MOE, D), f32), pltpu.SemaphoreType.DMA(()),
                            pltpu.SemaphoreType.DMA(()), pltpu.SemaphoreType.DMA(())],
        ),
        out_shape=jax.ShapeDtypeStruct((P_ROWS, D), f32),
        compiler_params=_cparams(("arbitrary",)),
        name="moe_dispatch",
    )(pos_flat, pad_bounds, h2)


def _combine_kernel(pos_ref, y_ref, x1_ref, route_ref, gtp, gts, *rest, final):
    n_ep = 1 if final else 5
    ep, outs, (buf, sem) = rest[:n_ep], rest[n_ep:-2], rest[-2:]
    i = pl.program_id(0)
    nt = pl.num_programs(0)
    slot = i % 2

    def start_tile(t, s):
        def issue(r, carry):
            for k in range(2):
                p = pos_ref[2 * (t * TM_COMB + r) + k]
                pltpu.make_async_copy(y_ref.at[pl.ds(p, 1), :], buf.at[s, k, pl.ds(r, 1), :], sem.at[s]).start()
            return carry
        lax.fori_loop(0, TM_COMB, issue, 0)

    @pl.when(i == 0)
    def _():
        start_tile(0, 0)

    @pl.when(i + 1 < nt)
    def _():
        start_tile(i + 1, 1 - slot)

    for k in range(2):
        pltpu.make_async_copy(y_ref.at[pl.ds(0, TM_COMB), :], buf.at[slot, k], sem.at[slot]).wait()
    f = route_ref[:, 2:3] * buf[slot, 0] + route_ref[:, 3:4] * buf[slot, 1]
    x2 = x1_ref[...] + _sel_mod(i, TM_COMB, gtp, gts) * f
    _residual_out(i, TM_COMB, x2, ep, outs, final)


def _combine_call(l, pos_flat, y, x1, route, modp, mods, g_mix, g_final):
    final = l == DEPTH - 1
    gtp, gts = _mod_specs(l, 5, TM_COMB, 3)
    ep_ops, ep_specs, out_specs, out_shape = _residual_specs(l, TM_COMB, 2, g_mix, g_final)
    if ep_ops is None:
        ep_ops = [modp, mods, modp, mods, g_mix.reshape(DEPTH, 1, D)]
    return pl.pallas_call(
        functools.partial(_combine_kernel, final=final),
        grid_spec=pltpu.PrefetchScalarGridSpec(
            num_scalar_prefetch=1,
            grid=(N_TOK // TM_COMB,),
            in_specs=[
                pl.BlockSpec(memory_space=pl.ANY),
                pl.BlockSpec((TM_COMB, D), lambda i, pos: (i, 0)),
                pl.BlockSpec((TM_COMB, 128), lambda i, pos: (i, 0)),
                gtp, gts, *ep_specs,
            ],
            out_specs=out_specs,
            scratch_shapes=[
                pltpu.VMEM((2, 2, TM_COMB, D), f32),
                pltpu.SemaphoreType.DMA((2,)),
            ],
        ),
        out_shape=out_shape,
        compiler_params=_cparams(("arbitrary",)),
        name="moe_combine",
    )(pos_flat, y, x1, route, modp, mods, *ep_ops)


def _moe_layer(l, h2, x1, w_router, wg, wu, wd, modp, mods, g_mix, g_final):
    lm = l // 2
    wr_pad = jnp.pad(w_router[lm], ((0, 0), (0, 128 - N_EXPERTS)))
    route, cnt = _router_call(h2, wr_pad)
    expert = route[:, 0:2].astype(i32)
    rank = route[:, 4:6].astype(i32)
    counts = cnt[0, :N_EXPERTS].astype(i32)
    tiles = (counts + SUB_MOE - 1) // SUB_MOE
    tile_start = jnp.cumsum(tiles) - tiles
    row_start = tile_start * SUB_MOE
    pos_flat = (row_start[expert] + rank).reshape(-1)
    n_tiles = jnp.sum(tiles)
    tail = jnp.stack([n_tiles * SUB_MOE, MAX_TILES - n_tiles])
    pad_bounds = jnp.concatenate([row_start + counts, row_start + tiles * SUB_MOE, tail])
    groups = (tiles + NSUB_MOE - 1) // NSUB_MOE
    group_end = jnp.cumsum(groups)
    n_groups = group_end[-1]
    gid = jnp.arange(MAX_GROUPS, dtype=i32)
    gcl = jnp.minimum(gid, n_groups - 1)
    group_expert = jnp.sum((gcl[:, None] >= group_end[None, :]).astype(i32), axis=1)
    first = NSUB_MOE * (gcl - (group_end - groups)[group_expert])
    group_n = jnp.where(gid < n_groups, jnp.minimum(NSUB_MOE, tiles[group_expert] - first), 0)
    sub_start = (tile_start[group_expert] + first)[:, None] + jnp.arange(NSUB_MOE, dtype=i32)[None, :]
    xs = _dispatch_call(pos_flat, pad_bounds, h2)
    y = _group_ffn_call("moe_experts", lm, group_expert, group_n, (sub_start * SUB_MOE).reshape(-1), tail, xs,
                        wg, wu, wd, P_ROWS, SUB_MOE, NSUB_MOE, MAX_GROUPS)
    return _combine_call(l, pos_flat, y, x1, route, modp, mods, g_mix, g_final)


def kernel(x_prompt, x_sample, state_pool, state_conv, c_prompt, c_sample, w_ada, b_ada, g_mix, g_ffn, w_in, w_pool, pool_scale, w_dw, b_dw, conv_ln_g, conv_ln_b, sgu_ln_g, sgu_ln_b, w_spatial, b_spatial, w_out, w_ff_gate, w_ff_up, w_ff_down, w_router, w_exp_gate, w_exp_up, w_exp_down, g_final):
    c_all = jnp.concatenate([c_prompt, c_sample, jnp.zeros((N_SEQ_PAD - N_SEQ, D), f32)], axis=0)
    modp = _ada_call(c_all, w_ada, b_ada)
    mods = jnp.repeat(modp[:, BATCH:N_SEQ], DEC_SEQ, axis=1)
    hp_t = state_pool.transpose(0, 2, 1, 3)
    hc_t = state_conv.transpose(0, 2, 1, 3)
    bs_bcast = jnp.broadcast_to(b_spatial[:, :, :, None], (DEPTH, N_HEADS, CHUNK, CHUNK))
    ws_small = w_spatial[:, :, :DEC_SEQ, :DEC_SEQ].reshape(DEPTH, -1)
    bs_small = b_spatial[:, :, :DEC_SEQ].reshape(DEPTH, -1)
    w_in_bf = w_in.astype(bf16)

    x, h1 = _prenorm_call(x_prompt, x_sample, modp, mods, g_mix)
    pool_p, conv_p, pool_s, conv_s, chunk_v = [], [], [], [], []
    for l in range(DEPTH):
        proj = _inproj_call(l, h1, w_in_bf)
        proj_s = proj[N_PROMPT:].reshape(DEC_BATCH, DEC_SEQ, P_IN)
        mix_s, glu_s, v_s = _mixs_call(l, ws_small[l], bs_small[l], proj_s.transpose(1, 0, 2), hp_t, hc_t,
                                       w_pool, pool_scale, w_dw, b_dw, conv_ln_g, conv_ln_b, sgu_ln_g, sgu_ln_b)
        mixcat, npool, nconv = _mixp_call(l, proj, mix_s.transpose(1, 0, 2).reshape(N_SAMPLE, D), w_pool, pool_scale,
                                          w_dw, b_dw, conv_ln_g, conv_ln_b, sgu_ln_g, sgu_ln_b, w_spatial, bs_bcast)
        pool_p.append(npool[:, POOL_HALO - POOL_HIST:])
        conv_p.append(nconv[:, CONV_HALO - CONV_HIST:])
        pool_s.append(jnp.concatenate([state_pool[l][:, DEC_SEQ:], proj_s[:, :, :D_POOL]], axis=1))
        conv_s.append(jnp.concatenate([state_conv[l][:, DEC_SEQ:], glu_s.transpose(1, 0, 2)], axis=1))
        chunk_v.append(v_s.transpose(1, 0, 2))
        x1, h2 = _outproj_call(l, mixcat, w_out, x, modp, mods, g_ffn, f32)
        if l % 2 == 0:
            y = _dense_ffn_call(l, h2, w_ff_gate, w_ff_up, w_ff_down)
            out = _dense_residual_call(l, y, x1, modp, mods, g_mix, g_final)
        else:
            out = _moe_layer(l, h2, x1, w_router, w_exp_gate, w_exp_up, w_exp_down, modp, mods, g_mix, g_final)
        if l < DEPTH - 1:
            x, h1 = out
    y_prompt, y_sample = out
    return (y_prompt.reshape(BATCH, SEQ, D), y_sample.reshape(DEC_BATCH, DEC_SEQ, D),
            jnp.stack(pool_p), jnp.stack(conv_p), jnp.stack(pool_s), jnp.stack(conv_s), jnp.stack(chunk_v))
```

```python
import functools

import jax
import jax.numpy as jnp
from jax import lax
from jax.experimental import pallas as pl
from jax.experimental.pallas import tpu as pltpu

f32 = jnp.float32
bf16 = jnp.bfloat16
i32 = jnp.int32

D = 2048
BATCH = 4
SEQ = 2048
DEPTH = 4
DEC_BATCH = 128
DEC_SEQ = 4
PAST_LEN = 16384
D_POOL = 512
POOL_WINDOWS = (2, 4, 8, 16)
POOL_GROUP = 128
POOL_HIST = 15
D_CONV = 768
CONV_WIDTH = 31
CONV_HIST = 30
D_CHUNK = 768
CHUNK = 128
N_HEADS = 6
P_IN = 3584
D_FF = 5632
N_EXPERTS = 8
D_FF_EXPERT = 7168
EPS = 1e-6

N_PROMPT = BATCH * SEQ
N_SAMPLE = DEC_BATCH * DEC_SEQ
N_TOK = N_PROMPT + N_SAMPLE
N_SEQ = BATCH + DEC_BATCH
N_SEQ_PAD = N_SAMPLE + 8

C1 = D_POOL
C2 = C1 + D_CONV
C3 = C2 + D_CONV
C4 = C3 + D_CHUNK

VMEM_LIMIT = 56 * 1024 * 1024
TM = 512
TM_OUT = 256
TN_ADA = 1024
TT = 256
ROW_CHUNK = 64
POOL_HALO = 16
CONV_HALO = 32
SB = 32
TF_FFN = 256
SUB_MOE = 768
NSUB_MOE = 3
SUB_DENSE = N_TOK // 16
NSUB_DENSE = 4
TM_COMB = 256
MAX_TILES = (N_TOK * 2) // SUB_MOE + N_EXPERTS
MAX_GROUPS = (MAX_TILES + (NSUB_MOE - 1) * N_EXPERTS) // NSUB_MOE
P_ROWS = MAX_TILES * SUB_MOE


def _cparams(sem):
    return pltpu.CompilerParams(dimension_semantics=sem, vmem_limit_bytes=VMEM_LIMIT)


def _sigmoid(x):
    return jax.nn.sigmoid(x)


def _rms(x, g):
    return x * lax.rsqrt(jnp.mean(x * x, axis=-1, keepdims=True) + EPS) * g


def _layer_norm(y, g, b):
    mu = jnp.mean(y, axis=-1, keepdims=True)
    d = y - mu
    var = jnp.mean(d * d, axis=-1, keepdims=True)
    return d * lax.rsqrt(var + EPS) * g + b


def _sel_mod(i, tm, p_ref, s_ref):
    seq = jnp.minimum(i // (SEQ // tm), BATCH - 1)
    return jnp.where(i >= N_PROMPT // tm, s_ref[...], p_ref[pl.ds(seq, 1), :])


def _mod_specs(l, chunk, tm, nargs):
    npt = N_PROMPT // tm
    pblk = N_SAMPLE // 8
    if nargs == 1:
        mp = lambda i: (l, pblk, chunk)
        ms = lambda i: (l, jnp.maximum(i - npt, 0), chunk)
    elif nargs == 2:
        mp = lambda i, j: (l, pblk, chunk)
        ms = lambda i, j: (l, jnp.maximum(i - npt, 0), chunk)
    else:
        mp = lambda i, s: (l, pblk, chunk)
        ms = lambda i, s: (l, jnp.maximum(i - npt, 0), chunk)
    return pl.BlockSpec((None, 8, D), mp), pl.BlockSpec((None, tm, D), ms, pipeline_mode=pl.Buffered(1))


def _ada_kernel(c_ref, w_ref, b_ref, o_ref):
    c = c_ref[...]
    s = (c * _sigmoid(c)).astype(bf16)
    o_ref[...] = jnp.dot(s, w_ref[...].astype(bf16), preferred_element_type=f32) + b_ref[...]


def _ada_call(c_all, w_ada, b_ada):
    nj = 6 * D // TN_ADA
    return pl.pallas_call(
        _ada_kernel,
        grid=(DEPTH, nj),
        in_specs=[
            pl.BlockSpec((N_SEQ_PAD, D), lambda l, j: (0, 0)),
            pl.BlockSpec((None, D, TN_ADA), lambda l, j: (l, 0, j)),
            pl.BlockSpec((None, 1, TN_ADA), lambda l, j: (l, 0, j)),
        ],
        out_specs=pl.BlockSpec((None, N_SEQ_PAD, TN_ADA), lambda l, j: (l, 0, j)),
        out_shape=jax.ShapeDtypeStruct((DEPTH, N_SEQ_PAD, 6 * D), f32),
        compiler_params=_cparams(("arbitrary", "arbitrary")),
        name="ada_mod",
    )(c_all, w_ada, b_ada.reshape(DEPTH, 1, 6 * D))


def _modulated_norm(i, tm, x, g_ref, shp, shs, scp, scs):
    h = _rms(x, g_ref[...]) * (1.0 + _sel_mod(i, tm, scp, scs)) + _sel_mod(i, tm, shp, shs)
    return h.astype(bf16)


def _prenorm_kernel(xp_ref, xs_ref, shp, shs, scp, scs, g_ref, x_ref, h_ref):
    i = pl.program_id(0)
    x = jnp.where(i >= N_PROMPT // TM, xs_ref[...], xp_ref[...])
    x_ref[...] = x
    h_ref[...] = _modulated_norm(i, TM, x, g_ref, shp, shs, scp, scs)


def _prenorm_call(x_prompt, x_sample, modp, mods, g_mix):
    assert N_SAMPLE == TM
    shp, shs = _mod_specs(0, 0, TM, 1)
    scp, scs = _mod_specs(0, 1, TM, 1)
    npt = N_PROMPT // TM
    return pl.pallas_call(
        _prenorm_kernel,
        grid=(N_TOK // TM,),
        in_specs=[
            pl.BlockSpec((TM, D), lambda i: (jnp.minimum(i, npt - 1), 0)),
            pl.BlockSpec((TM, D), lambda i: (0, 0), pipeline_mode=pl.Buffered(1)),
            shp, shs, scp, scs,
            pl.BlockSpec((None, 1, D), lambda i: (0, 0, 0)),
        ],
        out_specs=[pl.BlockSpec((TM, D), lambda i: (i, 0)), pl.BlockSpec((TM, D), lambda i: (i, 0))],
        out_shape=[jax.ShapeDtypeStruct((N_TOK, D), f32), jax.ShapeDtypeStruct((N_TOK, D), bf16)],
        compiler_params=_cparams(("arbitrary",)),
        name="prenorm",
    )(x_prompt.reshape(N_PROMPT, D), x_sample.reshape(N_SAMPLE, D), modp, mods, modp, mods,
      g_mix.reshape(DEPTH, 1, D))


def _inproj_kernel(h_ref, w_ref, o_ref):
    o_ref[...] = jnp.dot(h_ref[...], w_ref[...], preferred_element_type=f32)


def _inproj_call(l, h1, w_in_bf):
    return pl.pallas_call(
        _inproj_kernel,
        grid=(N_TOK // TM,),
        in_specs=[
            pl.BlockSpec((TM, D), lambda i: (i, 0)),
            pl.BlockSpec((None, D, P_IN), lambda i: (l, 0, 0), pipeline_mode=pl.Buffered(1)),
        ],
        out_specs=pl.BlockSpec((TM, P_IN), lambda i: (i, 0)),
        out_shape=jax.ShapeDtypeStruct((N_TOK, P_IN), f32),
        compiler_params=_cparams(("arbitrary",)),
        name="in_proj",
    )(h1, w_in_bf)


def _mixp_kernel(p_ref, ms_ref, wpool_ref, pscale_ref, wdw_ref, bdw_ref, clg_ref, clb_ref, slg_ref, slb_ref,
                 ws_ref, bs_ref, o_ref, *state_and_scratch):
    b = pl.program_id(0)

    @pl.when(b < BATCH)
    def _():
        _mixp_body(p_ref, wpool_ref, pscale_ref, wdw_ref, bdw_ref, clg_ref, clb_ref, slg_ref, slb_ref,
                   ws_ref, bs_ref, o_ref, *state_and_scratch)

    @pl.when(jnp.logical_and(b == BATCH, pl.program_id(1) < N_SAMPLE // TT))
    def _():
        o_ref[...] = ms_ref[...]


def _mixp_body(p_ref, wpool_ref, pscale_ref, wdw_ref, bdw_ref, clg_ref, clb_ref, slg_ref, slb_ref,
               ws_ref, bs_ref, o_ref, npool_ref, nconv_ref, pext, cext, ybuf):
    t = pl.program_id(1)

    @pl.when(t == 0)
    def _():
        pext[0:POOL_HALO, :] = jnp.zeros((POOL_HALO, D_POOL), f32)
        cext[0:CONV_HALO, :] = jnp.zeros((CONV_HALO, D_CONV), f32)

    pext[POOL_HALO:POOL_HALO + TT, :] = p_ref[:, 0:C1]
    pos = lax.broadcasted_iota(i32, (TT, POOL_GROUP), 0) + t * TT
    for g, w in enumerate(POOL_WINDOWS):
        sl = slice(g * POOL_GROUP, (g + 1) * POOL_GROUP)
        tok = pext[POOL_HALO:POOL_HALO + TT, sl]
        s = tok
        for j in range(1, w):
            s = s + pext[POOL_HALO - j:POOL_HALO - j + TT, sl]
        cnt = jnp.minimum(pos + 1, w).astype(f32)
        diff = (s / cnt - tok).astype(bf16)
        o = jnp.dot(diff, wpool_ref[g].astype(bf16), preferred_element_type=f32) * pscale_ref[:, sl]
        o_ref[:, sl] = o.astype(bf16)
    tail = pext[TT:TT + POOL_HALO, :]
    npool_ref[...] = tail
    pext[0:POOL_HALO, :] = tail

    cext[CONV_HALO:CONV_HALO + TT, :] = p_ref[:, C1:C2] * _sigmoid(p_ref[:, C2:C3])
    off = CONV_HALO - CONV_HIST
    for c in range(D_CONV // 128):
        cs = slice(c * 128, (c + 1) * 128)
        for r in range(TT // ROW_CHUNK):
            r0 = r * ROW_CHUNK
            acc = bdw_ref[:, cs]
            for res in range(8):
                rows = ROW_CHUNK if res == 0 else ROW_CHUNK + 8
                z = None
                for a in range((CONV_WIDTH + off) // 8 + 1):
                    k = 8 * a + res - off
                    if 0 <= k < CONV_WIDTH:
                        term = cext[r0 + 8 * a:r0 + 8 * a + rows, cs] * wdw_ref[k:k + 1, cs]
                        z = term if z is None else z + term
                acc = acc + (z if res == 0 else pltpu.roll(z, rows - res, axis=0)[0:ROW_CHUNK])
            ybuf[r0:r0 + ROW_CHUNK, cs] = acc
    yn = _layer_norm(ybuf[...], clg_ref[...], clb_ref[...])
    o_ref[:, C1:C2] = (yn * _sigmoid(yn)).astype(bf16)
    ctail = cext[TT:TT + CONV_HALO, :]
    nconv_ref[...] = ctail
    cext[0:CONV_HALO, :] = ctail

    ybuf[...] = _layer_norm(jax.nn.gelu(p_ref[:, C4:P_IN]), slg_ref[...], slb_ref[...])
    causal = (lax.broadcasted_iota(i32, (CHUNK, CHUNK), 0) >= lax.broadcasted_iota(i32, (CHUNK, CHUNK), 1))
    for h in range(N_HEADS):
        hs = slice(h * 128, (h + 1) * 128)
        wm = jnp.where(causal, ws_ref[h], 0.0).astype(bf16)
        for n in range(TT // CHUNK):
            rs = slice(n * CHUNK, (n + 1) * CHUNK)
            s = jnp.dot(wm, ybuf[rs, hs].astype(bf16), preferred_element_type=f32) + bs_ref[h]
            u = jax.nn.gelu(p_ref[rs, C3 + h * 128:C3 + (h + 1) * 128])
            o_ref[rs, C2 + h * 128:C2 + (h + 1) * 128] = (u * s).astype(bf16)


def _mixp_call(l, proj, mix_sample, w_pool, pool_scale, w_dw, b_dw, clg, clb, slg, slb, w_spatial, bs_bcast):
    nt = SEQ // TT
    last = N_TOK // TT - 1
    c2 = lambda b, t: (l, 0, 0)
    tile = lambda b, t: (jnp.minimum(b * nt + t, last), 0)
    seq = lambda b, t: (jnp.minimum(b, BATCH - 1), 0, 0)
    return pl.pallas_call(
        _mixp_kernel,
        grid=(BATCH + 1, nt),
        in_specs=[
            pl.BlockSpec((TT, P_IN), tile),
            pl.BlockSpec((TT, D), lambda b, t: (jnp.where(b == BATCH, jnp.minimum(t, N_SAMPLE // TT - 1), 0), 0)),
            pl.BlockSpec((None, 4, POOL_GROUP, POOL_GROUP), lambda b, t: (l, 0, 0, 0)),
            pl.BlockSpec((None, 1, D_POOL), c2),
            pl.BlockSpec((None, CONV_WIDTH, D_CONV), c2),
            pl.BlockSpec((None, 1, D_CONV), c2),
            pl.BlockSpec((None, 1, D_CONV), c2),
            pl.BlockSpec((None, 1, D_CONV), c2),
            pl.BlockSpec((None, 1, D_CHUNK), c2),
            pl.BlockSpec((None, 1, D_CHUNK), c2),
            pl.BlockSpec((None, N_HEADS, CHUNK, CHUNK), lambda b, t: (l, 0, 0, 0)),
            pl.BlockSpec((None, N_HEADS, CHUNK, CHUNK), lambda b, t: (l, 0, 0, 0)),
        ],
        out_specs=[
            pl.BlockSpec((TT, D), tile),
            pl.BlockSpec((None, POOL_HALO, D_POOL), seq),
            pl.BlockSpec((None, CONV_HALO, D_CONV), seq),
        ],
        out_shape=[
            jax.ShapeDtypeStruct((N_TOK, D), bf16),
            jax.ShapeDtypeStruct((BATCH, POOL_HALO, D_POOL), f32),
            jax.ShapeDtypeStruct((BATCH, CONV_HALO, D_CONV), f32),
        ],
        scratch_shapes=[
            pltpu.VMEM((TT + POOL_HALO, D_POOL), f32),
            pltpu.VMEM((TT + CONV_HALO, D_CONV), f32),
            pltpu.VMEM((TT, D_CONV), f32),
        ],
        compiler_params=_cparams(("arbitrary", "arbitrary")),
        name="mix_prompt",
    )(proj, mix_sample, w_pool, pool_scale.reshape(DEPTH, 1, D_POOL), w_dw, b_dw.reshape(DEPTH, 1, D_CONV),
      clg.reshape(DEPTH, 1, D_CONV), clb.reshape(DEPTH, 1, D_CONV), slg.reshape(DEPTH, 1, D_CHUNK),
      slb.reshape(DEPTH, 1, D_CHUNK), w_spatial, bs_bcast)


def _mixs_kernel(ws_ref, bs_ref, p_ref, hp_ref, hc_ref, wpool_ref, pscale_ref, wdw_ref, bdw_ref,
                 clg_ref, clb_ref, slg_ref, slb_ref, o_ref, glu_ref, v_ref):
    def pool_row(idx, sl):
        if idx < POOL_HIST:
            return hp_ref[idx, :, sl]
        return p_ref[idx - POOL_HIST, :, sl]

    for g, w in enumerate(POOL_WINDOWS):
        sl = slice(g * POOL_GROUP, (g + 1) * POOL_GROUP)
        diffs = []
        for t in range(DEC_SEQ):
            tok = pool_row(POOL_HIST + t, sl)
            s = tok
            for j in range(1, w):
                s = s + pool_row(POOL_HIST + t - j, sl)
            diffs.append(s * (1.0 / w) - tok)
        d = jnp.concatenate(diffs, axis=0).astype(bf16)
        o = jnp.dot(d, wpool_ref[g].astype(bf16), preferred_element_type=f32) * pscale_ref[:, sl]
        for t in range(DEC_SEQ):
            o_ref[t, :, sl] = o[t * SB:(t + 1) * SB].astype(bf16)

    for t in range(DEC_SEQ):
        glu_ref[t] = p_ref[t, :, C1:C2] * _sigmoid(p_ref[t, :, C2:C3])

    def conv_row(idx, cs):
        if idx < CONV_HIST:
            return hc_ref[idx, :, cs]
        return glu_ref[idx - CONV_HIST, :, cs]

    for t in range(DEC_SEQ):
        cols = []
        for c in range(D_CONV // 128):
            cs = slice(c * 128, (c + 1) * 128)
            acc = jnp.zeros((SB, 128), f32)
            for k in range(CONV_WIDTH):
                acc = acc + conv_row(t + k, cs) * wdw_ref[k:k + 1, cs]
            cols.append(acc + bdw_ref[:, cs])
        yn = _layer_norm(jnp.concatenate(cols, axis=-1), clg_ref[...], clb_ref[...])
        o_ref[t, :, C1:C2] = (yn * _sigmoid(yn)).astype(bf16)

    for t in range(DEC_SEQ):
        v_ref[t] = _layer_norm(jax.nn.gelu(p_ref[t, :, C4:P_IN]), slg_ref[...], slb_ref[...])
    for t in range(DEC_SEQ):
        for h in range(N_HEADS):
            hs = slice(h * 128, (h + 1) * 128)
            s = jnp.full((SB, 128), bs_ref[h * DEC_SEQ + t], f32)
            for k in range(t + 1):
                s = s + ws_ref[(h * DEC_SEQ + t) * DEC_SEQ + k] * v_ref[k, :, hs]
            u = jax.nn.gelu(p_ref[t, :, C3 + h * 128:C3 + (h + 1) * 128])
            o_ref[t, :, C2 + h * 128:C2 + (h + 1) * 128] = (u * s).astype(bf16)


def _mixs_call(l, ws_small, bs_small, proj_t, hp_t, hc_t, w_pool, pool_scale, w_dw, b_dw, clg, clb, slg, slb):
    c2 = lambda s: (l, 0, 0)
    return pl.pallas_call(
        _mixs_kernel,
        grid=(DEC_BATCH // SB,),
        in_specs=[
            pl.BlockSpec(memory_space=pltpu.SMEM),
            pl.BlockSpec(memory_space=pltpu.SMEM),
            pl.BlockSpec((DEC_SEQ, SB, P_IN), lambda s: (0, s, 0)),
            pl.BlockSpec((None, POOL_HIST, SB, D_POOL), lambda s: (l, 0, s, 0)),
            pl.BlockSpec((None, CONV_HIST, SB, D_CONV), lambda s: (l, 0, s, 0)),
            pl.BlockSpec((None, 4, POOL_GROUP, POOL_GROUP), lambda s: (l, 0, 0, 0)),
            pl.BlockSpec((None, 1, D_POOL), c2),
            pl.BlockSpec((None, CONV_WIDTH, D_CONV), c2),
            pl.BlockSpec((None, 1, D_CONV), c2),
            pl.BlockSpec((None, 1, D_CONV), c2),
            pl.BlockSpec((None, 1, D_CONV), c2),
            pl.BlockSpec((None, 1, D_CHUNK), c2),
            pl.BlockSpec((None, 1, D_CHUNK), c2),
        ],
        out_specs=[
            pl.BlockSpec((DEC_SEQ, SB, D), lambda s: (0, s, 0)),
            pl.BlockSpec((DEC_SEQ, SB, D_CONV), lambda s: (0, s, 0)),
            pl.BlockSpec((DEC_SEQ, SB, D_CHUNK), lambda s: (0, s, 0)),
        ],
        out_shape=[
            jax.ShapeDtypeStruct((DEC_SEQ, DEC_BATCH, D), bf16),
            jax.ShapeDtypeStruct((DEC_SEQ, DEC_BATCH, D_CONV), f32),
            jax.ShapeDtypeStruct((DEC_SEQ, DEC_BATCH, D_CHUNK), f32),
        ],
        compiler_params=_cparams(("arbitrary",)),
        name="mix_sample",
    )(ws_small, bs_small, proj_t, hp_t, hc_t, w_pool, pool_scale.reshape(DEPTH, 1, D_POOL), w_dw,
      b_dw.reshape(DEPTH, 1, D_CONV), clg.reshape(DEPTH, 1, D_CONV), clb.reshape(DEPTH, 1, D_CONV),
      slg.reshape(DEPTH, 1, D_CHUNK), slb.reshape(DEPTH, 1, D_CHUNK))


def _outproj_kernel(m_ref, w_ref, x_ref, gtp, gts, shp, shs, scp, scs, g_ref, x1_ref, h2_ref, wbf):
    i = pl.program_id(0)

    @pl.when(i == 0)
    def _():
        def cast_rows(r, carry):
            rs = pl.ds(pl.multiple_of(r * 256, 256), 256)
            wbf[rs, :] = w_ref[rs, :].astype(bf16)
            return carry
        lax.fori_loop(0, D // 256, cast_rows, 0)

    mix = jnp.dot(m_ref[...], wbf[...], preferred_element_type=f32)
    x1 = x_ref[...] + _sel_mod(i, TM_OUT, gtp, gts) * mix
    x1_ref[...] = x1
    h2 = _rms(x1, g_ref[...]) * (1.0 + _sel_mod(i, TM_OUT, scp, scs)) + _sel_mod(i, TM_OUT, shp, shs)
    h2_ref[...] = h2.astype(h2_ref.dtype)


def _outproj_call(l, mixcat, w_out, x, modp, mods, g_ffn, h2_dtype):
    gtp, gts = _mod_specs(l, 2, TM_OUT, 1)
    shp, shs = _mod_specs(l, 3, TM_OUT, 1)
    scp, scs = _mod_specs(l, 4, TM_OUT, 1)
    return pl.pallas_call(
        _outproj_kernel,
        grid=(N_TOK // TM_OUT,),
        in_specs=[
            pl.BlockSpec((TM_OUT, D), lambda i: (i, 0)),
            pl.BlockSpec((None, D, D), lambda i: (l, 0, 0), pipeline_mode=pl.Buffered(1)),
            pl.BlockSpec((TM_OUT, D), lambda i: (i, 0)),
            gtp, gts, shp, shs, scp, scs,
            pl.BlockSpec((None, 1, D), lambda i: (l, 0, 0)),
        ],
        out_specs=[
            pl.BlockSpec((TM_OUT, D), lambda i: (i, 0)),
            pl.BlockSpec((TM_OUT, D), lambda i: (i, 0)),
        ],
        out_shape=[
            jax.ShapeDtypeStruct((N_TOK, D), f32),
            jax.ShapeDtypeStruct((N_TOK, D), h2_dtype),
        ],
        scratch_shapes=[pltpu.VMEM((D, D), bf16)],
        compiler_params=_cparams(("arbitrary",)),
        name="out_proj",
    )(mixcat, w_out, x, modp, mods, modp, mods, modp, mods, g_ffn.reshape(DEPTH, 1, D))


def _group_ffn_kernel(ge_ref, gn_ref, rs_ref, tail_ref, x_hbm, wg_ref, wu_ref, wd_ref, y_hbm,
                      xbf, yacc, wgs, wus, wds, sem, *, sub, nsub, nj, all_full):
    del ge_ref
    g = pl.program_id(0)
    j = pl.program_id(1)
    n = gn_ref[g]

    @pl.when(jnp.logical_and(g == 0, j == 0))
    def _():
        yacc[0] = jnp.zeros((sub, D), f32)

        def tail_copy(c):
            r0 = pl.multiple_of(tail_ref[0] + c * sub, 8)
            return pltpu.make_async_copy(yacc.at[0], y_hbm.at[pl.ds(r0, sub), :], sem.at[1, 0])

        def start(c, carry):
            tail_copy(c).start()
            return carry

        def wait(c, carry):
            tail_copy(c).wait()
            return carry

        lax.fori_loop(0, tail_ref[1], start, 0)
        lax.fori_loop(0, tail_ref[1], wait, 0)

    def x_copy(m):
        r0 = pl.multiple_of(rs_ref[g * nsub + m], 8)
        return pltpu.make_async_copy(x_hbm.at[pl.ds(r0, sub), :], yacc.at[m], sem.at[0, m])

    def y_copy(m):
        r0 = pl.multiple_of(rs_ref[g * nsub + m], 8)
        return pltpu.make_async_copy(yacc.at[m], y_hbm.at[pl.ds(r0, sub), :], sem.at[1, m])

    def for_each_subtile(fn):
        def body(m, carry):
            fn(m)
            return carry
        lax.fori_loop(0, n, body, 0)

    @pl.when(jnp.logical_and(n > 0, j == 0))
    def _():
        for_each_subtile(lambda m: x_copy(m).start())

        def land(m):
            x_copy(m).wait()
            xbf[m] = yacc[m].astype(bf16)
            yacc[m] = jnp.zeros((sub, D), f32)
        for_each_subtile(land)

    def swiglu_slice(m, wg, wu, wd):
        x = xbf[m]
        gate = jnp.dot(x, wg, preferred_element_type=f32)
        up = jnp.dot(x, wu, preferred_element_type=f32)
        act = (gate * _sigmoid(gate) * up).astype(bf16)
        yacc[m] += jnp.dot(act, wd, preferred_element_type=f32)

    def narrow_weights_and_first_subtile():
        wg = wg_ref[...].astype(bf16)
        wu = wu_ref[...].astype(bf16)
        wd = wd_ref[...].astype(bf16)
        wgs[...] = wg
        wus[...] = wu
        wds[...] = wd
        swiglu_slice(0, wg, wu, wd)

    @pl.when(n == nsub)
    def _():
        narrow_weights_and_first_subtile()
        for m in range(1, nsub):
            swiglu_slice(m, wgs[...], wus[...], wds[...])

    if not all_full:
        @pl.when(jnp.logical_and(n > 0, n < nsub))
        def _():
            narrow_weights_and_first_subtile()

            def rest(m, carry):
                swiglu_slice(m, wgs[...], wus[...], wds[...])
                return carry
            lax.fori_loop(1, n, rest, 0)

    @pl.when(jnp.logical_and(n > 0, j == nj - 1))
    def _():
        for_each_subtile(lambda m: y_copy(m).start())
        for_each_subtile(lambda m: y_copy(m).wait())


def _group_ffn_call(name, layer, group_expert, group_n, row_start, tail, x, wg, wu, wd, out_rows, sub, nsub,
                    n_groups, all_full):
    d_ff = wg.shape[-1]
    nj = d_ff // TF_FFN

    def jmap(g, j, gn):
        return jnp.where(gn[g] > 0, j, nj - 1)

    return pl.pallas_call(
        functools.partial(_group_ffn_kernel, sub=sub, nsub=nsub, nj=nj, all_full=all_full),
        grid_spec=pltpu.PrefetchScalarGridSpec(
            num_scalar_prefetch=4,
            grid=(n_groups, nj),
            in_specs=[
                pl.BlockSpec(memory_space=pl.ANY),
                pl.BlockSpec((None, None, D, TF_FFN), lambda g, j, ge, gn, rs, tl: (layer, ge[g], 0, jmap(g, j, gn))),
                pl.BlockSpec((None, None, D, TF_FFN), lambda g, j, ge, gn, rs, tl: (layer, ge[g], 0, jmap(g, j, gn))),
                pl.BlockSpec((None, None, TF_FFN, D), lambda g, j, ge, gn, rs, tl: (layer, ge[g], jmap(g, j, gn), 0)),
            ],
            out_specs=pl.BlockSpec(memory_space=pl.ANY),
            scratch_shapes=[
                pltpu.VMEM((nsub, sub, D), bf16),
                pltpu.VMEM((nsub, sub, D), f32),
                pltpu.VMEM((D, TF_FFN), bf16),
                pltpu.VMEM((D, TF_FFN), bf16),
                pltpu.VMEM((TF_FFN, D), bf16),
                pltpu.SemaphoreType.DMA((2, nsub)),
            ],
        ),
        out_shape=jax.ShapeDtypeStruct((out_rows, D), f32),
        compiler_params=_cparams(("arbitrary", "arbitrary")),
        name=name,
    )(group_expert, group_n, row_start, tail, x, wg, wu, wd)


def _dense_ffn_call(l, h2, wg, wu, wd):
    n_groups = N_TOK // (SUB_DENSE * NSUB_DENSE)
    zeros = jnp.zeros((n_groups,), i32)
    return _group_ffn_call(
        "ffn_dense", l // 2, zeros, zeros + NSUB_DENSE, jnp.arange(n_groups * NSUB_DENSE, dtype=i32) * SUB_DENSE,
        jnp.zeros((2,), i32), h2, wg[:, None], wu[:, None], wd[:, None], N_TOK, SUB_DENSE, NSUB_DENSE, n_groups,
        True)


def _residual_out(i, tm, x2, ep, outs, final):
    if final:
        (g_ref,), (yp_ref, ys_ref) = ep, outs
        y = _rms(x2, g_ref[...])
        npt = N_PROMPT // tm

        @pl.when(i < npt)
        def _():
            yp_ref[...] = y

        @pl.when(i >= npt)
        def _():
            ys_ref[...] = y
    else:
        (shp, shs, scp, scs, g_ref), (x_ref, h_ref) = ep, outs
        x_ref[...] = x2
        h_ref[...] = _modulated_norm(i, tm, x2, g_ref, shp, shs, scp, scs)


def _residual_specs(l, tm, nargs, g_mix, g_final):
    if nargs == 1:
        tile = lambda i: (i, 0)
        const3 = lambda i: (l + 1, 0, 0)
        const2 = lambda i: (0, 0)
        pmap = lambda i: (jnp.minimum(i, N_PROMPT // tm - 1), 0)
        smap = lambda i: (jnp.maximum(i - N_PROMPT // tm, 0), 0)
    else:
        tile = lambda i, s: (i, 0)
        const3 = lambda i, s: (l + 1, 0, 0)
        const2 = lambda i, s: (0, 0)
        pmap = lambda i, s: (jnp.minimum(i, N_PROMPT // tm - 1), 0)
        smap = lambda i, s: (jnp.maximum(i - N_PROMPT // tm, 0), 0)
    if l == DEPTH - 1:
        return ([g_final.reshape(1, D)], [pl.BlockSpec((1, D), const2)],
                [pl.BlockSpec((tm, D), pmap), pl.BlockSpec((tm, D), smap)],
                [jax.ShapeDtypeStruct((N_PROMPT, D), f32), jax.ShapeDtypeStruct((N_SAMPLE, D), f32)])
    mod_nargs = 1 if nargs == 1 else 3
    return (None, [*_mod_specs(l + 1, 0, tm, mod_nargs), *_mod_specs(l + 1, 1, tm, mod_nargs),
                   pl.BlockSpec((None, 1, D), const3)],
            [pl.BlockSpec((tm, D), tile), pl.BlockSpec((tm, D), tile)],
            [jax.ShapeDtypeStruct((N_TOK, D), f32), jax.ShapeDtypeStruct((N_TOK, D), bf16)])


def _dense_residual_kernel(y_ref, x1_ref, gtp, gts, *rest, final):
    i = pl.program_id(0)
    n_ep = 1 if final else 5
    x2 = x1_ref[...] + _sel_mod(i, TM, gtp, gts) * y_ref[...]
    _residual_out(i, TM, x2, rest[:n_ep], rest[n_ep:], final)


def _dense_residual_call(l, y, x1, modp, mods, g_mix, g_final):
    final = l == DEPTH - 1
    gtp, gts = _mod_specs(l, 5, TM, 1)
    ep_ops, ep_specs, out_specs, out_shape = _residual_specs(l, TM, 1, g_mix, g_final)
    if ep_ops is None:
        ep_ops = [modp, mods, modp, mods, g_mix.reshape(DEPTH, 1, D)]
    return pl.pallas_call(
        functools.partial(_dense_residual_kernel, final=final),
        grid=(N_TOK // TM,),
        in_specs=[pl.BlockSpec((TM, D), lambda i: (i, 0)), pl.BlockSpec((TM, D), lambda i: (i, 0)), gtp, gts, *ep_specs],
        out_specs=out_specs,
        out_shape=out_shape,
        compiler_params=_cparams(("arbitrary",)),
        name="ffn_residual",
    )(y, x1, modp, mods, *ep_ops)


def _router_kernel(h_ref, wr_ref, route_ref, cnt_ref, carry):
    i = pl.program_id(0)

    @pl.when(i == 0)
    def _():
        carry[...] = jnp.zeros((8, 128), f32)

    h = h_ref[...]
    hh = h.astype(bf16)
    hl = (h - hh.astype(f32)).astype(bf16)
    w = wr_ref[...]
    wh = w.astype(bf16)
    wl = (w - wh.astype(f32)).astype(bf16)
    logits = (jnp.dot(hh, wh, preferred_element_type=f32) + jnp.dot(hl, wh, preferred_element_type=f32)
              + jnp.dot(hh, wl, preferred_element_type=f32))
    lane = lax.broadcasted_iota(i32, (TM, 128), 1)
    neg = jnp.float32(-jnp.inf)
    lg = jnp.where(lane < N_EXPERTS, logits, neg)
    m1 = jnp.max(lg, axis=-1, keepdims=True)
    i1 = jnp.min(jnp.where(lg == m1, lane, 128), axis=-1, keepdims=True)
    lg2 = jnp.where(lane == i1, neg, lg)
    m2 = jnp.max(lg2, axis=-1, keepdims=True)
    i2 = jnp.min(jnp.where(lg2 == m2, lane, 128), axis=-1, keepdims=True)
    e2 = jnp.exp(m2 - m1)
    g1 = 1.0 / (1.0 + e2)
    g2 = e2 / (1.0 + e2)

    hit1 = lane == i1
    hit2 = lane == i2
    cnt = (hit1 | hit2).astype(f32)
    before = (lax.broadcasted_iota(i32, (TM, TM), 0) > lax.broadcasted_iota(i32, (TM, TM), 1))
    prefix = jnp.dot(before.astype(bf16), cnt.astype(bf16), preferred_element_type=f32) + carry[0:1, :]
    r1 = jnp.sum(jnp.where(hit1, prefix, 0.0), axis=-1, keepdims=True)
    r2 = jnp.sum(jnp.where(hit2, prefix, 0.0), axis=-1, keepdims=True)
    total = carry[...] + jnp.sum(cnt, axis=0, keepdims=True)
    carry[...] = total
    cnt_ref[...] = total

    out = jnp.where(lane == 0, i1.astype(f32), 0.0)
    out = jnp.where(lane == 1, i2.astype(f32), out)
    out = jnp.where(lane == 2, g1, out)
    out = jnp.where(lane == 3, g2, out)
    out = jnp.where(lane == 4, r1, out)
    out = jnp.where(lane == 5, r2, out)
    route_ref[...] = out


def _router_call(h2, wr_pad):
    return pl.pallas_call(
        _router_kernel,
        grid=(N_TOK // TM,),
        in_specs=[
            pl.BlockSpec((TM, D), lambda i: (i, 0)),
            pl.BlockSpec((D, 128), lambda i: (0, 0)),
        ],
        out_specs=[
            pl.BlockSpec((TM, 128), lambda i: (i, 0)),
            pl.BlockSpec((8, 128), lambda i: (0, 0)),
        ],
        out_shape=[
            jax.ShapeDtypeStruct((N_TOK, 128), f32),
            jax.ShapeDtypeStruct((8, 128), f32),
        ],
        scratch_shapes=[pltpu.VMEM((8, 128), f32)],
        compiler_params=_cparams(("arbitrary",)),
        name="moe_router",
    )(h2, wr_pad)


def _dispatch_kernel(pos_ref, pad_ref, h_ref, xs_ref, zbuf, sem, zsem, tsem):
    i = pl.program_id(0)

    def zero_copy(r):
        return pltpu.make_async_copy(zbuf.at[pl.ds(0, 1), :], xs_ref.at[pl.ds(r, 1), :], zsem)

    def tail_copy(c):
        r0 = pl.multiple_of(pad_ref[2 * N_EXPERTS] + c * SUB_MOE, 8)
        return pltpu.make_async_copy(zbuf, xs_ref.at[pl.ds(r0, SUB_MOE), :], tsem)

    def for_each_pad_row(fn):
        for e in range(N_EXPERTS):
            def body(r, carry):
                fn(r)
                return carry
            lax.fori_loop(pad_ref[e], pad_ref[N_EXPERTS + e], body, 0)

    def for_each_tail_tile(fn):
        def body(c, carry):
            fn(c)
            return carry
        lax.fori_loop(0, pad_ref[2 * N_EXPERTS + 1], body, 0)

    @pl.when(i == 0)
    def _():
        zbuf[...] = jnp.zeros((SUB_MOE, D), f32)
        for_each_pad_row(lambda r: zero_copy(r).start())
        for_each_tail_tile(lambda c: tail_copy(c).start())

    def row_copy(r, slot):
        p = pos_ref[2 * (i * TM + r) + slot]
        return pltpu.make_async_copy(h_ref.at[pl.ds(r, 1), :], xs_ref.at[pl.ds(p, 1), :], sem)

    def issue(r, carry):
        row_copy(r, 0).start(priority=0)
        row_copy(r, 1).start(priority=1)
        return carry

    lax.fori_loop(0, TM, issue, 0)
    for _ in range(2):
        pltpu.make_async_copy(h_ref, xs_ref.at[pl.ds(0, TM), :], sem).wait()

    @pl.when(i == 0)
    def _():
        for_each_pad_row(lambda r: zero_copy(r).wait())
        for_each_tail_tile(lambda c: tail_copy(c).wait())


def _dispatch_call(pos_flat, pad_bounds, h2):
    return pl.pallas_call(
        _dispatch_kernel,
        grid_spec=pltpu.PrefetchScalarGridSpec(
            num_scalar_prefetch=2,
            grid=(N_TOK // TM,),
            in_specs=[pl.BlockSpec((TM, D), lambda i, pos, pad: (i, 0))],
            out_specs=pl.BlockSpec(memory_space=pl.ANY),
            scratch_shapes=[pltpu.VMEM((SUB_MOE, D), f32), pltpu.SemaphoreType.DMA(()),
                            pltpu.SemaphoreType.DMA(()), pltpu.SemaphoreType.DMA(())],
        ),
        out_shape=jax.ShapeDtypeStruct((P_ROWS, D), f32),
        compiler_params=_cparams(("arbitrary",)),
        name="moe_dispatch",
    )(pos_flat, pad_bounds, h2)


def _combine_kernel(pos_ref, y_ref, x1_ref, route_ref, gtp, gts, *rest, final):
    n_ep = 1 if final else 5
    ep, outs, (buf, sem) = rest[:n_ep], rest[n_ep:-2], rest[-2:]
    i = pl.program_id(0)
    nt = pl.num_programs(0)
    slot = i % 2

    def start_tile(t, s):
        def issue(r, carry):
            for k in range(2):
                p = pos_ref[2 * (t * TM_COMB + r) + k]
                pltpu.make_async_copy(y_ref.at[pl.ds(p, 1), :], buf.at[s, k, pl.ds(r, 1), :],
                                      sem.at[s]).start(priority=k)
            return carry
        lax.fori_loop(0, TM_COMB, issue, 0)

    @pl.when(i == 0)
    def _():
        start_tile(0, 0)

    @pl.when(i + 1 < nt)
    def _():
        start_tile(i + 1, 1 - slot)

    for k in range(2):
        pltpu.make_async_copy(y_ref.at[pl.ds(0, TM_COMB), :], buf.at[slot, k], sem.at[slot]).wait()
    f = route_ref[:, 2:3] * buf[slot, 0] + route_ref[:, 3:4] * buf[slot, 1]
    x2 = x1_ref[...] + _sel_mod(i, TM_COMB, gtp, gts) * f
    _residual_out(i, TM_COMB, x2, ep, outs, final)


def _combine_call(l, pos_flat, y, x1, route, modp, mods, g_mix, g_final):
    final = l == DEPTH - 1
    gtp, gts = _mod_specs(l, 5, TM_COMB, 3)
    ep_ops, ep_specs, out_specs, out_shape = _residual_specs(l, TM_COMB, 2, g_mix, g_final)
    if ep_ops is None:
        ep_ops = [modp, mods, modp, mods, g_mix.reshape(DEPTH, 1, D)]
    return pl.pallas_call(
        functools.partial(_combine_kernel, final=final),
        grid_spec=pltpu.PrefetchScalarGridSpec(
            num_scalar_prefetch=1,
            grid=(N_TOK // TM_COMB,),
            in_specs=[
                pl.BlockSpec(memory_space=pl.ANY),
                pl.BlockSpec((TM_COMB, D), lambda i, pos: (i, 0)),
                pl.BlockSpec((TM_COMB, 128), lambda i, pos: (i, 0)),
                gtp, gts, *ep_specs,
            ],
            out_specs=out_specs,
            scratch_shapes=[
                pltpu.VMEM((2, 2, TM_COMB, D), f32),
                pltpu.SemaphoreType.DMA((2,)),
            ],
        ),
        out_shape=out_shape,
        compiler_params=_cparams(("arbitrary",)),
        name="moe_combine",
    )(pos_flat, y, x1, route, modp, mods, *ep_ops)


def _moe_layer(l, h2, x1, w_router, wg, wu, wd, modp, mods, g_mix, g_final):
    lm = l // 2
    wr_pad = jnp.pad(w_router[lm], ((0, 0), (0, 128 - N_EXPERTS)))
    route, cnt = _router_call(h2, wr_pad)
    expert = route[:, 0:2].astype(i32)
    rank = route[:, 4:6].astype(i32)
    counts = cnt[0, :N_EXPERTS].astype(i32)
    tiles = (counts + SUB_MOE - 1) // SUB_MOE
    tile_start = jnp.cumsum(tiles) - tiles
    row_start = tile_start * SUB_MOE
    pos_flat = (row_start[expert] + rank).reshape(-1)
    n_tiles = jnp.sum(tiles)
    tail = jnp.stack([n_tiles * SUB_MOE, MAX_TILES - n_tiles])
    pad_bounds = jnp.concatenate([row_start + counts, row_start + tiles * SUB_MOE, tail])
    groups = (tiles + NSUB_MOE - 1) // NSUB_MOE
    group_end = jnp.cumsum(groups)
    n_groups = group_end[-1]
    gid = jnp.arange(MAX_GROUPS, dtype=i32)
    gcl = jnp.minimum(gid, n_groups - 1)
    group_expert = jnp.sum((gcl[:, None] >= group_end[None, :]).astype(i32), axis=1)
    first = NSUB_MOE * (gcl - (group_end - groups)[group_expert])
    group_n = jnp.where(gid < n_groups, jnp.minimum(NSUB_MOE, tiles[group_expert] - first), 0)
    sub_start = (tile_start[group_expert] + first)[:, None] + jnp.arange(NSUB_MOE, dtype=i32)[None, :]
    xs = _dispatch_call(pos_flat, pad_bounds, h2)
    y = _group_ffn_call("moe_experts", lm, group_expert, group_n, (sub_start * SUB_MOE).reshape(-1), tail, xs,
                        wg, wu, wd, P_ROWS, SUB_MOE, NSUB_MOE, MAX_GROUPS, False)
    return _combine_call(l, pos_flat, y, x1, route, modp, mods, g_mix, g_final)


def kernel(x_prompt, x_sample, state_pool, state_conv, c_prompt, c_sample, w_ada, b_ada, g_mix, g_ffn, w_in, w_pool, pool_scale, w_dw, b_dw, conv_ln_g, conv_ln_b, sgu_ln_g, sgu_ln_b, w_spatial, b_spatial, w_out, w_ff_gate, w_ff_up, w_ff_down, w_router, w_exp_gate, w_exp_up, w_exp_down, g_final):
    c_all = jnp.concatenate([jnp.repeat(c_sample, DEC_SEQ, axis=0), c_prompt,
                             jnp.zeros((N_SEQ_PAD - N_SAMPLE - BATCH, D), f32)], axis=0)
    modp = mods = _ada_call(c_all, w_ada, b_ada)
    hp_t = state_pool.transpose(0, 2, 1, 3)
    hc_t = state_conv.transpose(0, 2, 1, 3)
    bs_bcast = jnp.broadcast_to(b_spatial[:, :, :, None], (DEPTH, N_HEADS, CHUNK, CHUNK))
    ws_small = w_spatial[:, :, :DEC_SEQ, :DEC_SEQ].reshape(DEPTH, -1)
    bs_small = b_spatial[:, :, :DEC_SEQ].reshape(DEPTH, -1)
    w_in_bf = w_in.astype(bf16)

    x, h1 = _prenorm_call(x_prompt, x_sample, modp, mods, g_mix)
    pool_p, conv_p, pool_s, conv_s, chunk_v = [], [], [], [], []
    for l in range(DEPTH):
        proj = _inproj_call(l, h1, w_in_bf)
        proj_s = proj[N_PROMPT:].reshape(DEC_BATCH, DEC_SEQ, P_IN)
        mix_s, glu_s, v_s = _mixs_call(l, ws_small[l], bs_small[l], proj_s.transpose(1, 0, 2), hp_t, hc_t,
                                       w_pool, pool_scale, w_dw, b_dw, conv_ln_g, conv_ln_b, sgu_ln_g, sgu_ln_b)
        mixcat, npool, nconv = _mixp_call(l, proj, mix_s.transpose(1, 0, 2).reshape(N_SAMPLE, D), w_pool, pool_scale,
                                          w_dw, b_dw, conv_ln_g, conv_ln_b, sgu_ln_g, sgu_ln_b, w_spatial, bs_bcast)
        pool_p.append(npool[:, POOL_HALO - POOL_HIST:])
        conv_p.append(nconv[:, CONV_HALO - CONV_HIST:])
        pool_s.append(jnp.concatenate([state_pool[l][:, DEC_SEQ:], proj_s[:, :, :D_POOL]], axis=1))
        conv_s.append(jnp.concatenate([state_conv[l][:, DEC_SEQ:], glu_s.transpose(1, 0, 2)], axis=1))
        chunk_v.append(v_s.transpose(1, 0, 2))
        x1, h2 = _outproj_call(l, mixcat, w_out, x, modp, mods, g_ffn, f32)
        if l % 2 == 0:
            y = _dense_ffn_call(l, h2, w_ff_gate, w_ff_up, w_ff_down)
            out = _dense_residual_call(l, y, x1, modp, mods, g_mix, g_final)
        else:
            out = _moe_layer(l, h2, x1, w_router, w_exp_gate, w_exp_up, w_exp_down, modp, mods, g_mix, g_final)
        if l < DEPTH - 1:
            x, h1 = out
    y_prompt, y_sample = out
    return (y_prompt.reshape(BATCH, SEQ, D), y_sample.reshape(DEC_BATCH, DEC_SEQ, D),
            jnp.stack(pool_p), jnp.stack(conv_p), jnp.stack(pool_s), jnp.stack(conv_s), jnp.stack(chunk_v))
```

```python
import functools

import jax
import jax.numpy as jnp
from jax import lax
from jax.experimental import pallas as pl
from jax.experimental.pallas import tpu as pltpu

f32 = jnp.float32
bf16 = jnp.bfloat16
i32 = jnp.int32

D = 2048
BATCH = 4
SEQ = 2048
DEPTH = 4
DEC_BATCH = 128
DEC_SEQ = 4
PAST_LEN = 16384
D_POOL = 512
POOL_WINDOWS = (2, 4, 8, 16)
POOL_GROUP = 128
POOL_HIST = 15
D_CONV = 768
CONV_WIDTH = 31
CONV_HIST = 30
D_CHUNK = 768
CHUNK = 128
N_HEADS = 6
P_IN = 3584
D_FF = 5632
N_EXPERTS = 8
D_FF_EXPERT = 7168
EPS = 1e-6

N_PROMPT = BATCH * SEQ
N_SAMPLE = DEC_BATCH * DEC_SEQ
N_TOK = N_PROMPT + N_SAMPLE
N_SEQ = BATCH + DEC_BATCH
N_SEQ_PAD = N_SAMPLE + 8

C1 = D_POOL
C2 = C1 + D_CONV
C3 = C2 + D_CONV
C4 = C3 + D_CHUNK

VMEM_LIMIT = 56 * 1024 * 1024
TM = 512
TM_OUT = 256
TN_ADA = 1024
TT = 256
ROW_CHUNK = 64
POOL_HALO = 16
CONV_HALO = 32
SB = 32
TF_FFN = 256
SUB_MOE = 256
NSUB_MOE = 11
CHUNK_MOE = 3
SUB_DENSE = N_TOK // 16
NSUB_DENSE = 4
TM_COMB = 256
MAX_TILES = (N_TOK * 2) // SUB_MOE + N_EXPERTS
MAX_GROUPS = (MAX_TILES + (NSUB_MOE - 1) * N_EXPERTS) // NSUB_MOE
P_ROWS = MAX_TILES * SUB_MOE


def _cparams(sem):
    return pltpu.CompilerParams(dimension_semantics=sem, vmem_limit_bytes=VMEM_LIMIT)


def _sigmoid(x):
    return jax.nn.sigmoid(x)


def _rms(x, g):
    return x * lax.rsqrt(jnp.mean(x * x, axis=-1, keepdims=True) + EPS) * g


def _layer_norm(y, g, b):
    mu = jnp.mean(y, axis=-1, keepdims=True)
    d = y - mu
    var = jnp.mean(d * d, axis=-1, keepdims=True)
    return d * lax.rsqrt(var + EPS) * g + b


def _sel_mod(i, tm, p_ref, s_ref):
    seq = jnp.minimum(i // (SEQ // tm), BATCH - 1)
    return jnp.where(i >= N_PROMPT // tm, s_ref[...], p_ref[pl.ds(seq, 1), :])


def _mod_specs(l, chunk, tm, nargs):
    npt = N_PROMPT // tm
    pblk = N_SAMPLE // 8
    if nargs == 1:
        mp = lambda i: (l, pblk, chunk)
        ms = lambda i: (l, jnp.maximum(i - npt, 0), chunk)
    elif nargs == 2:
        mp = lambda i, j: (l, pblk, chunk)
        ms = lambda i, j: (l, jnp.maximum(i - npt, 0), chunk)
    else:
        mp = lambda i, s: (l, pblk, chunk)
        ms = lambda i, s: (l, jnp.maximum(i - npt, 0), chunk)
    return pl.BlockSpec((None, 8, D), mp), pl.BlockSpec((None, tm, D), ms, pipeline_mode=pl.Buffered(1))


def _ada_kernel(c_ref, w_ref, b_ref, o_ref):
    c = c_ref[...]
    s = (c * _sigmoid(c)).astype(bf16)
    o_ref[...] = jnp.dot(s, w_ref[...].astype(bf16), preferred_element_type=f32) + b_ref[...]


def _ada_call(c_all, w_ada, b_ada):
    nj = 6 * D // TN_ADA
    return pl.pallas_call(
        _ada_kernel,
        grid=(DEPTH, nj),
        in_specs=[
            pl.BlockSpec((N_SEQ_PAD, D), lambda l, j: (0, 0)),
            pl.BlockSpec((None, D, TN_ADA), lambda l, j: (l, 0, j)),
            pl.BlockSpec((None, 1, TN_ADA), lambda l, j: (l, 0, j)),
        ],
        out_specs=pl.BlockSpec((None, N_SEQ_PAD, TN_ADA), lambda l, j: (l, 0, j)),
        out_shape=jax.ShapeDtypeStruct((DEPTH, N_SEQ_PAD, 6 * D), f32),
        compiler_params=_cparams(("arbitrary", "arbitrary")),
        name="ada_mod",
    )(c_all, w_ada, b_ada.reshape(DEPTH, 1, 6 * D))


def _modulated_norm(i, tm, x, g_ref, shp, shs, scp, scs):
    h = _rms(x, g_ref[...]) * (1.0 + _sel_mod(i, tm, scp, scs)) + _sel_mod(i, tm, shp, shs)
    return h.astype(bf16)


def _prenorm_kernel(xp_ref, xs_ref, shp, shs, scp, scs, g_ref, x_ref, h_ref):
    i = pl.program_id(0)
    x = jnp.where(i >= N_PROMPT // TM, xs_ref[...], xp_ref[...])
    x_ref[...] = x
    h_ref[...] = _modulated_norm(i, TM, x, g_ref, shp, shs, scp, scs)


def _prenorm_call(x_prompt, x_sample, modp, mods, g_mix):
    assert N_SAMPLE == TM
    shp, shs = _mod_specs(0, 0, TM, 1)
    scp, scs = _mod_specs(0, 1, TM, 1)
    npt = N_PROMPT // TM
    return pl.pallas_call(
        _prenorm_kernel,
        grid=(N_TOK // TM,),
        in_specs=[
            pl.BlockSpec((TM, D), lambda i: (jnp.minimum(i, npt - 1), 0)),
            pl.BlockSpec((TM, D), lambda i: (0, 0), pipeline_mode=pl.Buffered(1)),
            shp, shs, scp, scs,
            pl.BlockSpec((None, 1, D), lambda i: (0, 0, 0)),
        ],
        out_specs=[pl.BlockSpec((TM, D), lambda i: (i, 0)), pl.BlockSpec((TM, D), lambda i: (i, 0))],
        out_shape=[jax.ShapeDtypeStruct((N_TOK, D), f32), jax.ShapeDtypeStruct((N_TOK, D), bf16)],
        compiler_params=_cparams(("arbitrary",)),
        name="prenorm",
    )(x_prompt.reshape(N_PROMPT, D), x_sample.reshape(N_SAMPLE, D), modp, mods, modp, mods,
      g_mix.reshape(DEPTH, 1, D))


def _inproj_kernel(h_ref, w_ref, o_ref):
    o_ref[...] = jnp.dot(h_ref[...], w_ref[...], preferred_element_type=f32)


def _inproj_call(l, h1, w_in_bf):
    return pl.pallas_call(
        _inproj_kernel,
        grid=(N_TOK // TM,),
        in_specs=[
            pl.BlockSpec((TM, D), lambda i: (i, 0)),
            pl.BlockSpec((None, D, P_IN), lambda i: (l, 0, 0), pipeline_mode=pl.Buffered(1)),
        ],
        out_specs=pl.BlockSpec((TM, P_IN), lambda i: (i, 0)),
        out_shape=jax.ShapeDtypeStruct((N_TOK, P_IN), f32),
        compiler_params=_cparams(("arbitrary",)),
        name="in_proj",
    )(h1, w_in_bf)


def _mixp_kernel(p_ref, ms_ref, wpool_ref, pscale_ref, wdw_ref, bdw_ref, clg_ref, clb_ref, slg_ref, slb_ref,
                 ws_ref, bs_ref, o_ref, *state_and_scratch):
    b = pl.program_id(0)

    @pl.when(b < BATCH)
    def _():
        _mixp_body(p_ref, wpool_ref, pscale_ref, wdw_ref, bdw_ref, clg_ref, clb_ref, slg_ref, slb_ref,
                   ws_ref, bs_ref, o_ref, *state_and_scratch)

    @pl.when(jnp.logical_and(b == BATCH, pl.program_id(1) < N_SAMPLE // TT))
    def _():
        o_ref[...] = ms_ref[...]


def _mixp_body(p_ref, wpool_ref, pscale_ref, wdw_ref, bdw_ref, clg_ref, clb_ref, slg_ref, slb_ref,
               ws_ref, bs_ref, o_ref, npool_ref, nconv_ref, pext, cext, ybuf):
    t = pl.program_id(1)

    @pl.when(t == 0)
    def _():
        pext[0:POOL_HALO, :] = jnp.zeros((POOL_HALO, D_POOL), f32)
        cext[0:CONV_HALO, :] = jnp.zeros((CONV_HALO, D_CONV), f32)

    pext[POOL_HALO:POOL_HALO + TT, :] = p_ref[:, 0:C1]
    pos = lax.broadcasted_iota(i32, (TT, POOL_GROUP), 0) + t * TT
    for g, w in enumerate(POOL_WINDOWS):
        sl = slice(g * POOL_GROUP, (g + 1) * POOL_GROUP)
        tok = pext[POOL_HALO:POOL_HALO + TT, sl]
        s = tok
        for j in range(1, w):
            s = s + pext[POOL_HALO - j:POOL_HALO - j + TT, sl]
        cnt = jnp.minimum(pos + 1, w).astype(f32)
        diff = (s / cnt - tok).astype(bf16)
        o = jnp.dot(diff, wpool_ref[g].astype(bf16), preferred_element_type=f32) * pscale_ref[:, sl]
        o_ref[:, sl] = o.astype(bf16)
    tail = pext[TT:TT + POOL_HALO, :]
    npool_ref[...] = tail
    pext[0:POOL_HALO, :] = tail

    cext[CONV_HALO:CONV_HALO + TT, :] = p_ref[:, C1:C2] * _sigmoid(p_ref[:, C2:C3])
    off = CONV_HALO - CONV_HIST
    for c in range(D_CONV // 128):
        cs = slice(c * 128, (c + 1) * 128)
        for r in range(TT // ROW_CHUNK):
            r0 = r * ROW_CHUNK
            acc = bdw_ref[:, cs]
            for res in range(8):
                rows = ROW_CHUNK if res == 0 else ROW_CHUNK + 8
                z = None
                for a in range((CONV_WIDTH + off) // 8 + 1):
                    k = 8 * a + res - off
                    if 0 <= k < CONV_WIDTH:
                        term = cext[r0 + 8 * a:r0 + 8 * a + rows, cs] * wdw_ref[k:k + 1, cs]
                        z = term if z is None else z + term
                acc = acc + (z if res == 0 else pltpu.roll(z, rows - res, axis=0)[0:ROW_CHUNK])
            ybuf[r0:r0 + ROW_CHUNK, cs] = acc
    yn = _layer_norm(ybuf[...], clg_ref[...], clb_ref[...])
    o_ref[:, C1:C2] = (yn * _sigmoid(yn)).astype(bf16)
    ctail = cext[TT:TT + CONV_HALO, :]
    nconv_ref[...] = ctail
    cext[0:CONV_HALO, :] = ctail

    ybuf[...] = _layer_norm(jax.nn.gelu(p_ref[:, C4:P_IN]), slg_ref[...], slb_ref[...])
    causal = (lax.broadcasted_iota(i32, (CHUNK, CHUNK), 0) >= lax.broadcasted_iota(i32, (CHUNK, CHUNK), 1))
    for h in range(N_HEADS):
        hs = slice(h * 128, (h + 1) * 128)
        wm = jnp.where(causal, ws_ref[h], 0.0).astype(bf16)
        for n in range(TT // CHUNK):
            rs = slice(n * CHUNK, (n + 1) * CHUNK)
            s = jnp.dot(wm, ybuf[rs, hs].astype(bf16), preferred_element_type=f32) + bs_ref[h]
            u = jax.nn.gelu(p_ref[rs, C3 + h * 128:C3 + (h + 1) * 128])
            o_ref[rs, C2 + h * 128:C2 + (h + 1) * 128] = (u * s).astype(bf16)


def _mixp_call(l, proj, mix_sample, w_pool, pool_scale, w_dw, b_dw, clg, clb, slg, slb, w_spatial, bs_bcast):
    nt = SEQ // TT
    last = N_TOK // TT - 1
    c2 = lambda b, t: (l, 0, 0)
    tile = lambda b, t: (jnp.minimum(b * nt + t, last), 0)
    seq = lambda b, t: (jnp.minimum(b, BATCH - 1), 0, 0)
    return pl.pallas_call(
        _mixp_kernel,
        grid=(BATCH + 1, nt),
        in_specs=[
            pl.BlockSpec((TT, P_IN), tile),
            pl.BlockSpec((TT, D), lambda b, t: (jnp.where(b == BATCH, jnp.minimum(t, N_SAMPLE // TT - 1), 0), 0)),
            pl.BlockSpec((None, 4, POOL_GROUP, POOL_GROUP), lambda b, t: (l, 0, 0, 0)),
            pl.BlockSpec((None, 1, D_POOL), c2),
            pl.BlockSpec((None, CONV_WIDTH, D_CONV), c2),
            pl.BlockSpec((None, 1, D_CONV), c2),
            pl.BlockSpec((None, 1, D_CONV), c2),
            pl.BlockSpec((None, 1, D_CONV), c2),
            pl.BlockSpec((None, 1, D_CHUNK), c2),
            pl.BlockSpec((None, 1, D_CHUNK), c2),
            pl.BlockSpec((None, N_HEADS, CHUNK, CHUNK), lambda b, t: (l, 0, 0, 0)),
            pl.BlockSpec((None, N_HEADS, CHUNK, CHUNK), lambda b, t: (l, 0, 0, 0)),
        ],
        out_specs=[
            pl.BlockSpec((TT, D), tile),
            pl.BlockSpec((None, POOL_HALO, D_POOL), seq),
            pl.BlockSpec((None, CONV_HALO, D_CONV), seq),
        ],
        out_shape=[
            jax.ShapeDtypeStruct((N_TOK, D), bf16),
            jax.ShapeDtypeStruct((BATCH, POOL_HALO, D_POOL), f32),
            jax.ShapeDtypeStruct((BATCH, CONV_HALO, D_CONV), f32),
        ],
        scratch_shapes=[
            pltpu.VMEM((TT + POOL_HALO, D_POOL), f32),
            pltpu.VMEM((TT + CONV_HALO, D_CONV), f32),
            pltpu.VMEM((TT, D_CONV), f32),
        ],
        compiler_params=_cparams(("arbitrary", "arbitrary")),
        name="mix_prompt",
    )(proj, mix_sample, w_pool, pool_scale.reshape(DEPTH, 1, D_POOL), w_dw, b_dw.reshape(DEPTH, 1, D_CONV),
      clg.reshape(DEPTH, 1, D_CONV), clb.reshape(DEPTH, 1, D_CONV), slg.reshape(DEPTH, 1, D_CHUNK),
      slb.reshape(DEPTH, 1, D_CHUNK), w_spatial, bs_bcast)


def _mixs_kernel(ws_ref, bs_ref, p_ref, hp_ref, hc_ref, wpool_ref, pscale_ref, wdw_ref, bdw_ref,
                 clg_ref, clb_ref, slg_ref, slb_ref, o_ref, glu_ref, v_ref):
    def pool_row(idx, sl):
        if idx < POOL_HIST:
            return hp_ref[idx, :, sl]
        return p_ref[idx - POOL_HIST, :, sl]

    for g, w in enumerate(POOL_WINDOWS):
        sl = slice(g * POOL_GROUP, (g + 1) * POOL_GROUP)
        diffs = []
        for t in range(DEC_SEQ):
            tok = pool_row(POOL_HIST + t, sl)
            s = tok
            for j in range(1, w):
                s = s + pool_row(POOL_HIST + t - j, sl)
            diffs.append(s * (1.0 / w) - tok)
        d = jnp.concatenate(diffs, axis=0).astype(bf16)
        o = jnp.dot(d, wpool_ref[g].astype(bf16), preferred_element_type=f32) * pscale_ref[:, sl]
        for t in range(DEC_SEQ):
            o_ref[t, :, sl] = o[t * SB:(t + 1) * SB].astype(bf16)

    for t in range(DEC_SEQ):
        glu_ref[t] = p_ref[t, :, C1:C2] * _sigmoid(p_ref[t, :, C2:C3])

    def conv_row(idx, cs):
        if idx < CONV_HIST:
            return hc_ref[idx, :, cs]
        return glu_ref[idx - CONV_HIST, :, cs]

    for t in range(DEC_SEQ):
        cols = []
        for c in range(D_CONV // 128):
            cs = slice(c * 128, (c + 1) * 128)
            acc = jnp.zeros((SB, 128), f32)
            for k in range(CONV_WIDTH):
                acc = acc + conv_row(t + k, cs) * wdw_ref[k:k + 1, cs]
            cols.append(acc + bdw_ref[:, cs])
        yn = _layer_norm(jnp.concatenate(cols, axis=-1), clg_ref[...], clb_ref[...])
        o_ref[t, :, C1:C2] = (yn * _sigmoid(yn)).astype(bf16)

    for t in range(DEC_SEQ):
        v_ref[t] = _layer_norm(jax.nn.gelu(p_ref[t, :, C4:P_IN]), slg_ref[...], slb_ref[...])
    for t in range(DEC_SEQ):
        for h in range(N_HEADS):
            hs = slice(h * 128, (h + 1) * 128)
            s = jnp.full((SB, 128), bs_ref[h * DEC_SEQ + t], f32)
            for k in range(t + 1):
                s = s + ws_ref[(h * DEC_SEQ + t) * DEC_SEQ + k] * v_ref[k, :, hs]
            u = jax.nn.gelu(p_ref[t, :, C3 + h * 128:C3 + (h + 1) * 128])
            o_ref[t, :, C2 + h * 128:C2 + (h + 1) * 128] = (u * s).astype(bf16)


def _mixs_call(l, ws_small, bs_small, proj_t, hp_t, hc_t, w_pool, pool_scale, w_dw, b_dw, clg, clb, slg, slb):
    c2 = lambda s: (l, 0, 0)
    return pl.pallas_call(
        _mixs_kernel,
        grid=(DEC_BATCH // SB,),
        in_specs=[
            pl.BlockSpec(memory_space=pltpu.SMEM),
            pl.BlockSpec(memory_space=pltpu.SMEM),
            pl.BlockSpec((DEC_SEQ, SB, P_IN), lambda s: (0, s, 0)),
            pl.BlockSpec((None, POOL_HIST, SB, D_POOL), lambda s: (l, 0, s, 0)),
            pl.BlockSpec((None, CONV_HIST, SB, D_CONV), lambda s: (l, 0, s, 0)),
            pl.BlockSpec((None, 4, POOL_GROUP, POOL_GROUP), lambda s: (l, 0, 0, 0)),
            pl.BlockSpec((None, 1, D_POOL), c2),
            pl.BlockSpec((None, CONV_WIDTH, D_CONV), c2),
            pl.BlockSpec((None, 1, D_CONV), c2),
            pl.BlockSpec((None, 1, D_CONV), c2),
            pl.BlockSpec((None, 1, D_CONV), c2),
            pl.BlockSpec((None, 1, D_CHUNK), c2),
            pl.BlockSpec((None, 1, D_CHUNK), c2),
        ],
        out_specs=[
            pl.BlockSpec((DEC_SEQ, SB, D), lambda s: (0, s, 0)),
            pl.BlockSpec((DEC_SEQ, SB, D_CONV), lambda s: (0, s, 0)),
            pl.BlockSpec((DEC_SEQ, SB, D_CHUNK), lambda s: (0, s, 0)),
        ],
        out_shape=[
            jax.ShapeDtypeStruct((DEC_SEQ, DEC_BATCH, D), bf16),
            jax.ShapeDtypeStruct((DEC_SEQ, DEC_BATCH, D_CONV), f32),
            jax.ShapeDtypeStruct((DEC_SEQ, DEC_BATCH, D_CHUNK), f32),
        ],
        compiler_params=_cparams(("arbitrary",)),
        name="mix_sample",
    )(ws_small, bs_small, proj_t, hp_t, hc_t, w_pool, pool_scale.reshape(DEPTH, 1, D_POOL), w_dw,
      b_dw.reshape(DEPTH, 1, D_CONV), clg.reshape(DEPTH, 1, D_CONV), clb.reshape(DEPTH, 1, D_CONV),
      slg.reshape(DEPTH, 1, D_CHUNK), slb.reshape(DEPTH, 1, D_CHUNK))


def _outproj_kernel(m_ref, w_ref, x_ref, gtp, gts, shp, shs, scp, scs, g_ref, x1_ref, h2_ref, wbf):
    i = pl.program_id(0)

    @pl.when(i == 0)
    def _():
        def cast_rows(r, carry):
            rs = pl.ds(pl.multiple_of(r * 256, 256), 256)
            wbf[rs, :] = w_ref[rs, :].astype(bf16)
            return carry
        lax.fori_loop(0, D // 256, cast_rows, 0)

    mix = jnp.dot(m_ref[...], wbf[...], preferred_element_type=f32)
    x1 = x_ref[...] + _sel_mod(i, TM_OUT, gtp, gts) * mix
    x1_ref[...] = x1
    h2 = _rms(x1, g_ref[...]) * (1.0 + _sel_mod(i, TM_OUT, scp, scs)) + _sel_mod(i, TM_OUT, shp, shs)
    h2_ref[...] = h2.astype(h2_ref.dtype)


def _outproj_call(l, mixcat, w_out, x, modp, mods, g_ffn, h2_dtype):
    gtp, gts = _mod_specs(l, 2, TM_OUT, 1)
    shp, shs = _mod_specs(l, 3, TM_OUT, 1)
    scp, scs = _mod_specs(l, 4, TM_OUT, 1)
    return pl.pallas_call(
        _outproj_kernel,
        grid=(N_TOK // TM_OUT,),
        in_specs=[
            pl.BlockSpec((TM_OUT, D), lambda i: (i, 0)),
            pl.BlockSpec((None, D, D), lambda i: (l, 0, 0), pipeline_mode=pl.Buffered(1)),
            pl.BlockSpec((TM_OUT, D), lambda i: (i, 0)),
            gtp, gts, shp, shs, scp, scs,
            pl.BlockSpec((None, 1, D), lambda i: (l, 0, 0)),
        ],
        out_specs=[
            pl.BlockSpec((TM_OUT, D), lambda i: (i, 0)),
            pl.BlockSpec((TM_OUT, D), lambda i: (i, 0)),
        ],
        out_shape=[
            jax.ShapeDtypeStruct((N_TOK, D), f32),
            jax.ShapeDtypeStruct((N_TOK, D), h2_dtype),
        ],
        scratch_shapes=[pltpu.VMEM((D, D), bf16)],
        compiler_params=_cparams(("arbitrary",)),
        name="out_proj",
    )(mixcat, w_out, x, modp, mods, modp, mods, modp, mods, g_ffn.reshape(DEPTH, 1, D))


def _group_ffn_kernel(ge_ref, gn_ref, rs_ref, tail_ref, x_hbm, wg_ref, wu_ref, wd_ref, y_hbm,
                      xbf, yacc, wgs, wus, wds, sem, *, sub, nsub, nj, all_full, chunk):
    del ge_ref
    g = pl.program_id(0)
    j = pl.program_id(1)
    n = gn_ref[g]

    @pl.when(jnp.logical_and(g == 0, j == 0))
    def _():
        yacc[0] = jnp.zeros((sub, D), f32)

        def tail_copy(c):
            r0 = pl.multiple_of(tail_ref[0] + c * sub, 8)
            return pltpu.make_async_copy(yacc.at[0], y_hbm.at[pl.ds(r0, sub), :], sem.at[1, 0])

        def start(c, carry):
            tail_copy(c).start()
            return carry

        def wait(c, carry):
            tail_copy(c).wait()
            return carry

        lax.fori_loop(0, tail_ref[1], start, 0)
        lax.fori_loop(0, tail_ref[1], wait, 0)

    def x_copy(m):
        r0 = pl.multiple_of(rs_ref[g * nsub + m], 8)
        return pltpu.make_async_copy(x_hbm.at[pl.ds(r0, sub), :], yacc.at[m], sem.at[0, m])

    def y_copy(m):
        r0 = pl.multiple_of(rs_ref[g * nsub + m], 8)
        return pltpu.make_async_copy(yacc.at[m], y_hbm.at[pl.ds(r0, sub), :], sem.at[1, m])

    def for_each_subtile(fn):
        def body(m, carry):
            fn(m)
            return carry
        lax.fori_loop(0, n, body, 0)

    @pl.when(jnp.logical_and(n > 0, j == 0))
    def _():
        for_each_subtile(lambda m: x_copy(m).start())

        def land(m):
            x_copy(m).wait()
            xbf[m] = yacc[m].astype(bf16)
            yacc[m] = jnp.zeros((sub, D), f32)
        for_each_subtile(land)

    def swiglu_rows(m0, k, wg, wu, wd):
        x = xbf[pl.ds(m0, k)].reshape(k * sub, D)
        gate = jnp.dot(x, wg, preferred_element_type=f32)
        up = jnp.dot(x, wu, preferred_element_type=f32)
        act = (gate * _sigmoid(gate) * up).astype(bf16)
        yacc[pl.ds(m0, k)] += jnp.dot(act, wd, preferred_element_type=f32).reshape(k, sub, D)

    def narrow_weights_and_first(k):
        wg = wg_ref[...].astype(bf16)
        wu = wu_ref[...].astype(bf16)
        wd = wd_ref[...].astype(bf16)
        wgs[...] = wg
        wus[...] = wu
        wds[...] = wd
        swiglu_rows(0, k, wg, wu, wd)

    if all_full:
        @pl.when(n > 0)
        def _():
            narrow_weights_and_first(1)
            for m in range(1, nsub):
                swiglu_rows(m, 1, wgs[...], wus[...], wds[...])
    else:
        n_wide = n // chunk
        n_tail = n - chunk * n_wide

        @pl.when(n_wide > 0)
        def _():
            narrow_weights_and_first(chunk)

            def wide(t, carry):
                swiglu_rows(t * chunk, chunk, wgs[...], wus[...], wds[...])
                return carry
            lax.fori_loop(1, n_wide, wide, 0)
            for k in range(1, chunk):
                @pl.when(n_tail == k)
                def _():
                    swiglu_rows(n_wide * chunk, k, wgs[...], wus[...], wds[...])

        for k in range(1, chunk):
            @pl.when(jnp.logical_and(n_wide == 0, n_tail == k))
            def _():
                narrow_weights_and_first(k)

    @pl.when(jnp.logical_and(n > 0, j == nj - 1))
    def _():
        for_each_subtile(lambda m: y_copy(m).start())
        for_each_subtile(lambda m: y_copy(m).wait())


def _group_ffn_call(name, layer, group_expert, group_n, row_start, tail, x, wg, wu, wd, out_rows, sub, nsub,
                    n_groups, all_full, chunk=1):
    d_ff = wg.shape[-1]
    nj = d_ff // TF_FFN

    def jmap(g, j, gn):
        return jnp.where(gn[g] > 0, j, nj - 1)

    return pl.pallas_call(
        functools.partial(_group_ffn_kernel, sub=sub, nsub=nsub, nj=nj, all_full=all_full, chunk=chunk),
        grid_spec=pltpu.PrefetchScalarGridSpec(
            num_scalar_prefetch=4,
            grid=(n_groups, nj),
            in_specs=[
                pl.BlockSpec(memory_space=pl.ANY),
                pl.BlockSpec((None, None, D, TF_FFN), lambda g, j, ge, gn, rs, tl: (layer, ge[g], 0, jmap(g, j, gn))),
                pl.BlockSpec((None, None, D, TF_FFN), lambda g, j, ge, gn, rs, tl: (layer, ge[g], 0, jmap(g, j, gn))),
                pl.BlockSpec((None, None, TF_FFN, D), lambda g, j, ge, gn, rs, tl: (layer, ge[g], jmap(g, j, gn), 0)),
            ],
            out_specs=pl.BlockSpec(memory_space=pl.ANY),
            scratch_shapes=[
                pltpu.VMEM((nsub, sub, D), bf16),
                pltpu.VMEM((nsub, sub, D), f32),
                pltpu.VMEM((D, TF_FFN), bf16),
                pltpu.VMEM((D, TF_FFN), bf16),
                pltpu.VMEM((TF_FFN, D), bf16),
                pltpu.SemaphoreType.DMA((2, nsub)),
            ],
        ),
        out_shape=jax.ShapeDtypeStruct((out_rows, D), f32),
        compiler_params=_cparams(("arbitrary", "arbitrary")),
        name=name,
    )(group_expert, group_n, row_start, tail, x, wg, wu, wd)


def _dense_ffn_call(l, h2, wg, wu, wd):
    n_groups = N_TOK // (SUB_DENSE * NSUB_DENSE)
    zeros = jnp.zeros((n_groups,), i32)
    return _group_ffn_call(
        "ffn_dense", l // 2, zeros, zeros + NSUB_DENSE, jnp.arange(n_groups * NSUB_DENSE, dtype=i32) * SUB_DENSE,
        jnp.zeros((2,), i32), h2, wg[:, None], wu[:, None], wd[:, None], N_TOK, SUB_DENSE, NSUB_DENSE, n_groups,
        True)


def _residual_out(i, tm, x2, ep, outs, final):
    if final:
        (g_ref,), (yp_ref, ys_ref) = ep, outs
        y = _rms(x2, g_ref[...])
        npt = N_PROMPT // tm

        @pl.when(i < npt)
        def _():
            yp_ref[...] = y

        @pl.when(i >= npt)
        def _():
            ys_ref[...] = y
    else:
        (shp, shs, scp, scs, g_ref), (x_ref, h_ref) = ep, outs
        x_ref[...] = x2
        h_ref[...] = _modulated_norm(i, tm, x2, g_ref, shp, shs, scp, scs)


def _residual_specs(l, tm, nargs, g_mix, g_final):
    if nargs == 1:
        tile = lambda i: (i, 0)
        const3 = lambda i: (l + 1, 0, 0)
        const2 = lambda i: (0, 0)
        pmap = lambda i: (jnp.minimum(i, N_PROMPT // tm - 1), 0)
        smap = lambda i: (jnp.maximum(i - N_PROMPT // tm, 0), 0)
    else:
        tile = lambda i, s: (i, 0)
        const3 = lambda i, s: (l + 1, 0, 0)
        const2 = lambda i, s: (0, 0)
        pmap = lambda i, s: (jnp.minimum(i, N_PROMPT // tm - 1), 0)
        smap = lambda i, s: (jnp.maximum(i - N_PROMPT // tm, 0), 0)
    if l == DEPTH - 1:
        return ([g_final.reshape(1, D)], [pl.BlockSpec((1, D), const2)],
                [pl.BlockSpec((tm, D), pmap), pl.BlockSpec((tm, D), smap)],
                [jax.ShapeDtypeStruct((N_PROMPT, D), f32), jax.ShapeDtypeStruct((N_SAMPLE, D), f32)])
    mod_nargs = 1 if nargs == 1 else 3
    return (None, [*_mod_specs(l + 1, 0, tm, mod_nargs), *_mod_specs(l + 1, 1, tm, mod_nargs),
                   pl.BlockSpec((None, 1, D), const3)],
            [pl.BlockSpec((tm, D), tile), pl.BlockSpec((tm, D), tile)],
            [jax.ShapeDtypeStruct((N_TOK, D), f32), jax.ShapeDtypeStruct((N_TOK, D), bf16)])


def _dense_residual_kernel(y_ref, x1_ref, gtp, gts, *rest, final):
    i = pl.program_id(0)
    n_ep = 1 if final else 5
    x2 = x1_ref[...] + _sel_mod(i, TM, gtp, gts) * y_ref[...]
    _residual_out(i, TM, x2, rest[:n_ep], rest[n_ep:], final)


def _dense_residual_call(l, y, x1, modp, mods, g_mix, g_final):
    final = l == DEPTH - 1
    gtp, gts = _mod_specs(l, 5, TM, 1)
    ep_ops, ep_specs, out_specs, out_shape = _residual_specs(l, TM, 1, g_mix, g_final)
    if ep_ops is None:
        ep_ops = [modp, mods, modp, mods, g_mix.reshape(DEPTH, 1, D)]
    return pl.pallas_call(
        functools.partial(_dense_residual_kernel, final=final),
        grid=(N_TOK // TM,),
        in_specs=[pl.BlockSpec((TM, D), lambda i: (i, 0)), pl.BlockSpec((TM, D), lambda i: (i, 0)), gtp, gts, *ep_specs],
        out_specs=out_specs,
        out_shape=out_shape,
        compiler_params=_cparams(("arbitrary",)),
        name="ffn_residual",
    )(y, x1, modp, mods, *ep_ops)


def _router_kernel(h_ref, wr_ref, route_ref, cnt_ref, carry):
    i = pl.program_id(0)

    @pl.when(i == 0)
    def _():
        carry[...] = jnp.zeros((8, 128), f32)

    h = h_ref[...]
    hh = h.astype(bf16)
    hl = (h - hh.astype(f32)).astype(bf16)
    w = wr_ref[...]
    wh = w.astype(bf16)
    wl = (w - wh.astype(f32)).astype(bf16)
    logits = (jnp.dot(hh, wh, preferred_element_type=f32) + jnp.dot(hl, wh, preferred_element_type=f32)
              + jnp.dot(hh, wl, preferred_element_type=f32))
    lane = lax.broadcasted_iota(i32, (TM, 128), 1)
    neg = jnp.float32(-jnp.inf)
    lg = jnp.where(lane < N_EXPERTS, logits, neg)
    m1 = jnp.max(lg, axis=-1, keepdims=True)
    i1 = jnp.min(jnp.where(lg == m1, lane, 128), axis=-1, keepdims=True)
    lg2 = jnp.where(lane == i1, neg, lg)
    m2 = jnp.max(lg2, axis=-1, keepdims=True)
    i2 = jnp.min(jnp.where(lg2 == m2, lane, 128), axis=-1, keepdims=True)
    e2 = jnp.exp(m2 - m1)
    g1 = 1.0 / (1.0 + e2)
    g2 = e2 / (1.0 + e2)

    hit1 = lane == i1
    hit2 = lane == i2
    cnt = (hit1 | hit2).astype(f32)
    before = (lax.broadcasted_iota(i32, (TM, TM), 0) > lax.broadcasted_iota(i32, (TM, TM), 1))
    prefix = jnp.dot(before.astype(bf16), cnt.astype(bf16), preferred_element_type=f32) + carry[0:1, :]
    r1 = jnp.sum(jnp.where(hit1, prefix, 0.0), axis=-1, keepdims=True)
    r2 = jnp.sum(jnp.where(hit2, prefix, 0.0), axis=-1, keepdims=True)
    total = carry[...] + jnp.sum(cnt, axis=0, keepdims=True)
    carry[...] = total
    cnt_ref[...] = total

    out = jnp.where(lane == 0, i1.astype(f32), 0.0)
    out = jnp.where(lane == 1, i2.astype(f32), out)
    out = jnp.where(lane == 2, g1, out)
    out = jnp.where(lane == 3, g2, out)
    out = jnp.where(lane == 4, r1, out)
    out = jnp.where(lane == 5, r2, out)
    route_ref[...] = out


def _router_call(h2, wr_pad):
    return pl.pallas_call(
        _router_kernel,
        grid=(N_TOK // TM,),
        in_specs=[
            pl.BlockSpec((TM, D), lambda i: (i, 0)),
            pl.BlockSpec((D, 128), lambda i: (0, 0)),
        ],
        out_specs=[
            pl.BlockSpec((TM, 128), lambda i: (i, 0)),
            pl.BlockSpec((8, 128), lambda i: (0, 0)),
        ],
        out_shape=[
            jax.ShapeDtypeStruct((N_TOK, 128), f32),
            jax.ShapeDtypeStruct((8, 128), f32),
        ],
        scratch_shapes=[pltpu.VMEM((8, 128), f32)],
        compiler_params=_cparams(("arbitrary",)),
        name="moe_router",
    )(h2, wr_pad)


def _dispatch_kernel(pos_ref, pad_ref, h_ref, xs_ref, zbuf, sem, zsem, tsem):
    i = pl.program_id(0)

    def zero_copy(r):
        return pltpu.make_async_copy(zbuf.at[pl.ds(0, 1), :], xs_ref.at[pl.ds(r, 1), :], zsem)

    def tail_copy(c):
        r0 = pl.multiple_of(pad_ref[2 * N_EXPERTS] + c * SUB_MOE, 8)
        return pltpu.make_async_copy(zbuf, xs_ref.at[pl.ds(r0, SUB_MOE), :], tsem)

    def for_each_pad_row(fn):
        for e in range(N_EXPERTS):
            def body(r, carry):
                fn(r)
                return carry
            lax.fori_loop(pad_ref[e], pad_ref[N_EXPERTS + e], body, 0)

    def for_each_tail_tile(fn):
        def body(c, carry):
            fn(c)
            return carry
        lax.fori_loop(0, pad_ref[2 * N_EXPERTS + 1], body, 0)

    @pl.when(i == 0)
    def _():
        zbuf[...] = jnp.zeros((SUB_MOE, D), f32)
        for_each_pad_row(lambda r: zero_copy(r).start())
        for_each_tail_tile(lambda c: tail_copy(c).start())

    def row_copy(r, slot):
        p = pos_ref[2 * (i * TM + r) + slot]
        return pltpu.make_async_copy(h_ref.at[pl.ds(r, 1), :], xs_ref.at[pl.ds(p, 1), :], sem)

    def issue(r, carry):
        row_copy(r, 0).start(priority=0)
        row_copy(r, 1).start(priority=1)
        return carry

    lax.fori_loop(0, TM, issue, 0, unroll=8)
    for _ in range(2):
        pltpu.make_async_copy(h_ref, xs_ref.at[pl.ds(0, TM), :], sem).wait()

    @pl.when(i == 0)
    def _():
        for_each_pad_row(lambda r: zero_copy(r).wait())
        for_each_tail_tile(lambda c: tail_copy(c).wait())


def _dispatch_call(pos_flat, pad_bounds, h2):
    return pl.pallas_call(
        _dispatch_kernel,
        grid_spec=pltpu.PrefetchScalarGridSpec(
            num_scalar_prefetch=2,
            grid=(N_TOK // TM,),
            in_specs=[pl.BlockSpec((TM, D), lambda i, pos, pad: (i, 0))],
            out_specs=pl.BlockSpec(memory_space=pl.ANY),
            scratch_shapes=[pltpu.VMEM((SUB_MOE, D), f32), pltpu.SemaphoreType.DMA(()),
                            pltpu.SemaphoreType.DMA(()), pltpu.SemaphoreType.DMA(())],
        ),
        out_shape=jax.ShapeDtypeStruct((P_ROWS, D), f32),
        compiler_params=_cparams(("arbitrary",)),
        name="moe_dispatch",
    )(pos_flat, pad_bounds, h2)


def _combine_kernel(pos_ref, y_ref, x1_ref, route_ref, gtp, gts, *rest, final):
    n_ep = 1 if final else 5
    ep, outs, (buf, sem) = rest[:n_ep], rest[n_ep:-2], rest[-2:]
    i = pl.program_id(0)
    nt = pl.num_programs(0)
    slot = i % 2

    def start_tile(t, s):
        def issue(r, carry):
            for k in range(2):
                p = pos_ref[2 * (t * TM_COMB + r) + k]
                pltpu.make_async_copy(y_ref.at[pl.ds(p, 1), :], buf.at[s, k, pl.ds(r, 1), :],
                                      sem.at[s]).start(priority=k)
            return carry
        lax.fori_loop(0, TM_COMB, issue, 0, unroll=8)

    @pl.when(i == 0)
    def _():
        start_tile(0, 0)

    @pl.when(i + 1 < nt)
    def _():
        start_tile(i + 1, 1 - slot)

    for k in range(2):
        pltpu.make_async_copy(y_ref.at[pl.ds(0, TM_COMB), :], buf.at[slot, k], sem.at[slot]).wait()
    f = route_ref[:, 2:3] * buf[slot, 0] + route_ref[:, 3:4] * buf[slot, 1]
    x2 = x1_ref[...] + _sel_mod(i, TM_COMB, gtp, gts) * f
    _residual_out(i, TM_COMB, x2, ep, outs, final)


def _combine_call(l, pos_flat, y, x1, route, modp, mods, g_mix, g_final):
    final = l == DEPTH - 1
    gtp, gts = _mod_specs(l, 5, TM_COMB, 3)
    ep_ops, ep_specs, out_specs, out_shape = _residual_specs(l, TM_COMB, 2, g_mix, g_final)
    if ep_ops is None:
        ep_ops = [modp, mods, modp, mods, g_mix.reshape(DEPTH, 1, D)]
    return pl.pallas_call(
        functools.partial(_combine_kernel, final=final),
        grid_spec=pltpu.PrefetchScalarGridSpec(
            num_scalar_prefetch=1,
            grid=(N_TOK // TM_COMB,),
            in_specs=[
                pl.BlockSpec(memory_space=pl.ANY),
                pl.BlockSpec((TM_COMB, D), lambda i, pos: (i, 0)),
                pl.BlockSpec((TM_COMB, 128), lambda i, pos: (i, 0)),
                gtp, gts, *ep_specs,
            ],
            out_specs=out_specs,
            scratch_shapes=[
                pltpu.VMEM((2, 2, TM_COMB, D), f32),
                pltpu.SemaphoreType.DMA((2,)),
            ],
        ),
        out_shape=out_shape,
        compiler_params=_cparams(("arbitrary",)),
        name="moe_combine",
    )(pos_flat, y, x1, route, modp, mods, *ep_ops)


def _moe_layer(l, h2, x1, w_router, wg, wu, wd, modp, mods, g_mix, g_final):
    lm = l // 2
    wr_pad = jnp.pad(w_router[lm], ((0, 0), (0, 128 - N_EXPERTS)))
    route, cnt = _router_call(h2, wr_pad)
    expert = route[:, 0:2].astype(i32)
    rank = route[:, 4:6].astype(i32)
    counts = cnt[0, :N_EXPERTS].astype(i32)
    tiles = (counts + SUB_MOE - 1) // SUB_MOE
    tile_start = jnp.cumsum(tiles) - tiles
    row_start = tile_start * SUB_MOE
    pos_flat = (row_start[expert] + rank).reshape(-1)
    n_tiles = jnp.sum(tiles)
    tail = jnp.stack([n_tiles * SUB_MOE, MAX_TILES - n_tiles])
    pad_bounds = jnp.concatenate([row_start + counts, row_start + tiles * SUB_MOE, tail])
    groups = (tiles + NSUB_MOE - 1) // NSUB_MOE
    group_end = jnp.cumsum(groups)
    n_groups = group_end[-1]
    gid = jnp.arange(MAX_GROUPS, dtype=i32)
    gcl = jnp.minimum(gid, n_groups - 1)
    group_expert = jnp.sum((gcl[:, None] >= group_end[None, :]).astype(i32), axis=1)
    per_group = tiles // jnp.maximum(groups, 1)
    extra = tiles - per_group * groups
    k = gcl - (group_end - groups)[group_expert]
    first = k * per_group[group_expert] + jnp.minimum(k, extra[group_expert])
    group_n = jnp.where(gid < n_groups, per_group[group_expert] + (k < extra[group_expert]).astype(i32), 0)
    sub_start = (tile_start[group_expert] + first)[:, None] + jnp.arange(NSUB_MOE, dtype=i32)[None, :]
    xs = _dispatch_call(pos_flat, pad_bounds, h2)
    y = _group_ffn_call("moe_experts", lm, group_expert, group_n, (sub_start * SUB_MOE).reshape(-1), tail, xs,
                        wg, wu, wd, P_ROWS, SUB_MOE, NSUB_MOE, MAX_GROUPS, False, CHUNK_MOE)
    return _combine_call(l, pos_flat, y, x1, route, modp, mods, g_mix, g_final)


def kernel(x_prompt, x_sample, state_pool, state_conv, c_prompt, c_sample, w_ada, b_ada, g_mix, g_ffn, w_in, w_pool, pool_scale, w_dw, b_dw, conv_ln_g, conv_ln_b, sgu_ln_g, sgu_ln_b, w_spatial, b_spatial, w_out, w_ff_gate, w_ff_up, w_ff_down, w_router, w_exp_gate, w_exp_up, w_exp_down, g_final):
    c_all = jnp.concatenate([jnp.repeat(c_sample, DEC_SEQ, axis=0), c_prompt,
                             jnp.zeros((N_SEQ_PAD - N_SAMPLE - BATCH, D), f32)], axis=0)
    modp = mods = _ada_call(c_all, w_ada, b_ada)
    hp_t = state_pool.transpose(0, 2, 1, 3)
    hc_t = state_conv.transpose(0, 2, 1, 3)
    bs_bcast = jnp.broadcast_to(b_spatial[:, :, :, None], (DEPTH, N_HEADS, CHUNK, CHUNK))
    ws_small = w_spatial[:, :, :DEC_SEQ, :DEC_SEQ].reshape(DEPTH, -1)
    bs_small = b_spatial[:, :, :DEC_SEQ].reshape(DEPTH, -1)
    w_in_bf = w_in.astype(bf16)

    x, h1 = _prenorm_call(x_prompt, x_sample, modp, mods, g_mix)
    pool_p, conv_p, pool_s, conv_s, chunk_v = [], [], [], [], []
    for l in range(DEPTH):
        proj = _inproj_call(l, h1, w_in_bf)
        proj_s = proj[N_PROMPT:].reshape(DEC_BATCH, DEC_SEQ, P_IN)
        mix_s, glu_s, v_s = _mixs_call(l, ws_small[l], bs_small[l], proj_s.transpose(1, 0, 2), hp_t, hc_t,
                                       w_pool, pool_scale, w_dw, b_dw, conv_ln_g, conv_ln_b, sgu_ln_g, sgu_ln_b)
        mixcat, npool, nconv = _mixp_call(l, proj, mix_s.transpose(1, 0, 2).reshape(N_SAMPLE, D), w_pool, pool_scale,
                                          w_dw, b_dw, conv_ln_g, conv_ln_b, sgu_ln_g, sgu_ln_b, w_spatial, bs_bcast)
        pool_p.append(npool[:, POOL_HALO - POOL_HIST:])
        conv_p.append(nconv[:, CONV_HALO - CONV_HIST:])
        pool_s.append(jnp.concatenate([state_pool[l][:, DEC_SEQ:], proj_s[:, :, :D_POOL]], axis=1))
        conv_s.append(jnp.concatenate([state_conv[l][:, DEC_SEQ:], glu_s.transpose(1, 0, 2)], axis=1))
        chunk_v.append(v_s.transpose(1, 0, 2))
        x1, h2 = _outproj_call(l, mixcat, w_out, x, modp, mods, g_ffn, f32)
        if l % 2 == 0:
            y = _dense_ffn_call(l, h2, w_ff_gate, w_ff_up, w_ff_down)
            out = _dense_residual_call(l, y, x1, modp, mods, g_mix, g_final)
        else:
            out = _moe_layer(l, h2, x1, w_router, w_exp_gate, w_exp_up, w_exp_down, modp, mods, g_mix, g_final)
        if l < DEPTH - 1:
            x, h1 = out
    y_prompt, y_sample = out
    return (y_prompt.reshape(BATCH, SEQ, D), y_sample.reshape(DEC_BATCH, DEC_SEQ, D),
            jnp.stack(pool_p), jnp.stack(conv_p), jnp.stack(pool_s), jnp.stack(conv_s), jnp.stack(chunk_v))
```

```python
import functools

import jax
import jax.numpy as jnp
from jax import lax
from jax.experimental import pallas as pl
from jax.experimental.pallas import tpu as pltpu

f32 = jnp.float32
bf16 = jnp.bfloat16
i32 = jnp.int32

D = 2048
BATCH = 4
SEQ = 2048
DEPTH = 4
DEC_BATCH = 128
DEC_SEQ = 4
PAST_LEN = 16384
D_POOL = 512
POOL_WINDOWS = (2, 4, 8, 16)
POOL_GROUP = 128
POOL_HIST = 15
D_CONV = 768
CONV_WIDTH = 31
CONV_HIST = 30
D_CHUNK = 768
CHUNK = 128
N_HEADS = 6
P_IN = 3584
D_FF = 5632
N_EXPERTS = 8
D_FF_EXPERT = 7168
EPS = 1e-6

N_PROMPT = BATCH * SEQ
N_SAMPLE = DEC_BATCH * DEC_SEQ
N_TOK = N_PROMPT + N_SAMPLE
N_SEQ = BATCH + DEC_BATCH
N_SEQ_PAD = N_SAMPLE + 8

C1 = D_POOL
C2 = C1 + D_CONV
C3 = C2 + D_CONV
C4 = C3 + D_CHUNK

VMEM_LIMIT = 56 * 1024 * 1024
TM = 512
TM_OUT = 256
TN_ADA = 1024
TT = 256
ROW_CHUNK = 64
POOL_HALO = 16
CONV_HALO = 32
SB = 32
TF_FFN = 256
SUB_MOE = 256
NSUB_MOE = 11
CHUNK_MOE = 3
SUB_DENSE = N_TOK // 16
NSUB_DENSE = 4
TM_COMB = 256
MAX_TILES = (N_TOK * 2) // SUB_MOE + N_EXPERTS
MAX_GROUPS = (MAX_TILES + (NSUB_MOE - 1) * N_EXPERTS) // NSUB_MOE
P_ROWS = MAX_TILES * SUB_MOE


def _cparams(sem):
    return pltpu.CompilerParams(dimension_semantics=sem, vmem_limit_bytes=VMEM_LIMIT)


def _sigmoid(x):
    return jax.nn.sigmoid(x)


def _rms(x, g):
    return x * lax.rsqrt(jnp.mean(x * x, axis=-1, keepdims=True) + EPS) * g


def _layer_norm(y, g, b):
    mu = jnp.mean(y, axis=-1, keepdims=True)
    d = y - mu
    var = jnp.mean(d * d, axis=-1, keepdims=True)
    return d * lax.rsqrt(var + EPS) * g + b


def _sel_mod(i, tm, p_ref, s_ref):
    seq = jnp.minimum(i // (SEQ // tm), BATCH - 1)
    return jnp.where(i >= N_PROMPT // tm, s_ref[...], p_ref[pl.ds(seq, 1), :])


def _mod_specs(l, chunk, tm, nargs):
    npt = N_PROMPT // tm
    pblk = N_SAMPLE // 8
    if nargs == 1:
        mp = lambda i: (l, pblk, chunk)
        ms = lambda i: (l, jnp.maximum(i - npt, 0), chunk)
    elif nargs == 2:
        mp = lambda i, j: (l, pblk, chunk)
        ms = lambda i, j: (l, jnp.maximum(i - npt, 0), chunk)
    else:
        mp = lambda i, s: (l, pblk, chunk)
        ms = lambda i, s: (l, jnp.maximum(i - npt, 0), chunk)
    return pl.BlockSpec((None, 8, D), mp), pl.BlockSpec((None, tm, D), ms, pipeline_mode=pl.Buffered(1))


def _ada_kernel(c_ref, w_ref, b_ref, o_ref):
    c = c_ref[...]
    s = (c * _sigmoid(c)).astype(bf16)
    o_ref[...] = jnp.dot(s, w_ref[...].astype(bf16), preferred_element_type=f32) + b_ref[...]


def _ada_call(c_all, w_ada, b_ada):
    nj = 6 * D // TN_ADA
    return pl.pallas_call(
        _ada_kernel,
        grid=(DEPTH, nj),
        in_specs=[
            pl.BlockSpec((N_SEQ_PAD, D), lambda l, j: (0, 0)),
            pl.BlockSpec((None, D, TN_ADA), lambda l, j: (l, 0, j)),
            pl.BlockSpec((None, 1, TN_ADA), lambda l, j: (l, 0, j)),
        ],
        out_specs=pl.BlockSpec((None, N_SEQ_PAD, TN_ADA), lambda l, j: (l, 0, j)),
        out_shape=jax.ShapeDtypeStruct((DEPTH, N_SEQ_PAD, 6 * D), f32),
        compiler_params=_cparams(("arbitrary", "arbitrary")),
        name="ada_mod",
    )(c_all, w_ada, b_ada.reshape(DEPTH, 1, 6 * D))


def _modulated_norm(i, tm, x, g_ref, shp, shs, scp, scs):
    h = _rms(x, g_ref[...]) * (1.0 + _sel_mod(i, tm, scp, scs)) + _sel_mod(i, tm, shp, shs)
    return h.astype(bf16)


def _prenorm_kernel(xp_ref, xs_ref, shp, shs, scp, scs, g_ref, x_ref, h_ref):
    i = pl.program_id(0)
    x = jnp.where(i >= N_PROMPT // TM, xs_ref[...], xp_ref[...])
    x_ref[...] = x
    h_ref[...] = _modulated_norm(i, TM, x, g_ref, shp, shs, scp, scs)


def _prenorm_call(x_prompt, x_sample, modp, mods, g_mix):
    assert N_SAMPLE == TM
    shp, shs = _mod_specs(0, 0, TM, 1)
    scp, scs = _mod_specs(0, 1, TM, 1)
    npt = N_PROMPT // TM
    return pl.pallas_call(
        _prenorm_kernel,
        grid=(N_TOK // TM,),
        in_specs=[
            pl.BlockSpec((TM, D), lambda i: (jnp.minimum(i, npt - 1), 0)),
            pl.BlockSpec((TM, D), lambda i: (0, 0), pipeline_mode=pl.Buffered(1)),
            shp, shs, scp, scs,
            pl.BlockSpec((None, 1, D), lambda i: (0, 0, 0)),
        ],
        out_specs=[pl.BlockSpec((TM, D), lambda i: (i, 0)), pl.BlockSpec((TM, D), lambda i: (i, 0))],
        out_shape=[jax.ShapeDtypeStruct((N_TOK, D), f32), jax.ShapeDtypeStruct((N_TOK, D), bf16)],
        compiler_params=_cparams(("arbitrary",)),
        name="prenorm",
    )(x_prompt.reshape(N_PROMPT, D), x_sample.reshape(N_SAMPLE, D), modp, mods, modp, mods,
      g_mix.reshape(DEPTH, 1, D))


def _inproj_kernel(h_ref, w_ref, o_ref):
    o_ref[...] = jnp.dot(h_ref[...], w_ref[...], preferred_element_type=f32)


def _inproj_call(l, h1, w_in_bf):
    return pl.pallas_call(
        _inproj_kernel,
        grid=(N_TOK // TM,),
        in_specs=[
            pl.BlockSpec((TM, D), lambda i: (i, 0)),
            pl.BlockSpec((None, D, P_IN), lambda i: (l, 0, 0), pipeline_mode=pl.Buffered(1)),
        ],
        out_specs=pl.BlockSpec((TM, P_IN), lambda i: (i, 0)),
        out_shape=jax.ShapeDtypeStruct((N_TOK, P_IN), f32),
        compiler_params=_cparams(("arbitrary",)),
        name="in_proj",
    )(h1, w_in_bf)


def _mixp_kernel(p_ref, ms_ref, wpool_ref, pscale_ref, wdw_ref, bdw_ref, clg_ref, clb_ref, slg_ref, slb_ref,
                 ws_ref, bs_ref, o_ref, *state_and_scratch):
    b = pl.program_id(0)

    @pl.when(b < BATCH)
    def _():
        _mixp_body(p_ref, wpool_ref, pscale_ref, wdw_ref, bdw_ref, clg_ref, clb_ref, slg_ref, slb_ref,
                   ws_ref, bs_ref, o_ref, *state_and_scratch)

    @pl.when(jnp.logical_and(b == BATCH, pl.program_id(1) < N_SAMPLE // TT))
    def _():
        o_ref[...] = ms_ref[...]


def _mixp_body(p_ref, wpool_ref, pscale_ref, wdw_ref, bdw_ref, clg_ref, clb_ref, slg_ref, slb_ref,
               ws_ref, bs_ref, o_ref, npool_ref, nconv_ref, pext, cext, ybuf):
    t = pl.program_id(1)

    @pl.when(t == 0)
    def _():
        pext[0:POOL_HALO, :] = jnp.zeros((POOL_HALO, D_POOL), f32)
        cext[0:CONV_HALO, :] = jnp.zeros((CONV_HALO, D_CONV), f32)

    pext[POOL_HALO:POOL_HALO + TT, :] = p_ref[:, 0:C1]
    pos = lax.broadcasted_iota(i32, (TT, POOL_GROUP), 0) + t * TT
    for g, w in enumerate(POOL_WINDOWS):
        sl = slice(g * POOL_GROUP, (g + 1) * POOL_GROUP)
        tok = pext[POOL_HALO:POOL_HALO + TT, sl]
        s = tok
        for j in range(1, w):
            s = s + pext[POOL_HALO - j:POOL_HALO - j + TT, sl]
        cnt = jnp.minimum(pos + 1, w).astype(f32)
        diff = (s / cnt - tok).astype(bf16)
        o = jnp.dot(diff, wpool_ref[g].astype(bf16), preferred_element_type=f32) * pscale_ref[:, sl]
        o_ref[:, sl] = o.astype(bf16)
    tail = pext[TT:TT + POOL_HALO, :]
    npool_ref[...] = tail
    pext[0:POOL_HALO, :] = tail

    cext[CONV_HALO:CONV_HALO + TT, :] = p_ref[:, C1:C2] * _sigmoid(p_ref[:, C2:C3])
    off = CONV_HALO - CONV_HIST
    for c in range(D_CONV // 128):
        cs = slice(c * 128, (c + 1) * 128)
        for r in range(TT // ROW_CHUNK):
            r0 = r * ROW_CHUNK
            acc = bdw_ref[:, cs]
            for res in range(8):
                rows = ROW_CHUNK if res == 0 else ROW_CHUNK + 8
                z = None
                for a in range((CONV_WIDTH + off) // 8 + 1):
                    k = 8 * a + res - off
                    if 0 <= k < CONV_WIDTH:
                        term = cext[r0 + 8 * a:r0 + 8 * a + rows, cs] * wdw_ref[k:k + 1, cs]
                        z = term if z is None else z + term
                acc = acc + (z if res == 0 else pltpu.roll(z, rows - res, axis=0)[0:ROW_CHUNK])
            ybuf[r0:r0 + ROW_CHUNK, cs] = acc
    yn = _layer_norm(ybuf[...], clg_ref[...], clb_ref[...])
    o_ref[:, C1:C2] = (yn * _sigmoid(yn)).astype(bf16)
    ctail = cext[TT:TT + CONV_HALO, :]
    nconv_ref[...] = ctail
    cext[0:CONV_HALO, :] = ctail

    ybuf[...] = _layer_norm(jax.nn.gelu(p_ref[:, C4:P_IN]), slg_ref[...], slb_ref[...])
    causal = (lax.broadcasted_iota(i32, (CHUNK, CHUNK), 0) >= lax.broadcasted_iota(i32, (CHUNK, CHUNK), 1))
    for h in range(N_HEADS):
        hs = slice(h * 128, (h + 1) * 128)
        wm = jnp.where(causal, ws_ref[h], 0.0).astype(bf16)
        for n in range(TT // CHUNK):
            rs = slice(n * CHUNK, (n + 1) * CHUNK)
            s = jnp.dot(wm, ybuf[rs, hs].astype(bf16), preferred_element_type=f32) + bs_ref[h]
            u = jax.nn.gelu(p_ref[rs, C3 + h * 128:C3 + (h + 1) * 128])
            o_ref[rs, C2 + h * 128:C2 + (h + 1) * 128] = (u * s).astype(bf16)


def _mixp_call(l, proj, mix_sample, w_pool, pool_scale, w_dw, b_dw, clg, clb, slg, slb, w_spatial, bs_bcast):
    nt = SEQ // TT
    last = N_TOK // TT - 1
    c2 = lambda b, t: (l, 0, 0)
    tile = lambda b, t: (jnp.minimum(b * nt + t, last), 0)
    seq = lambda b, t: (jnp.minimum(b, BATCH - 1), 0, 0)
    return pl.pallas_call(
        _mixp_kernel,
        grid=(BATCH + 1, nt),
        in_specs=[
            pl.BlockSpec((TT, P_IN), tile),
            pl.BlockSpec((TT, D), lambda b, t: (jnp.where(b == BATCH, jnp.minimum(t, N_SAMPLE // TT - 1), 0), 0)),
            pl.BlockSpec((None, 4, POOL_GROUP, POOL_GROUP), lambda b, t: (l, 0, 0, 0)),
            pl.BlockSpec((None, 1, D_POOL), c2),
            pl.BlockSpec((None, CONV_WIDTH, D_CONV), c2),
            pl.BlockSpec((None, 1, D_CONV), c2),
            pl.BlockSpec((None, 1, D_CONV), c2),
            pl.BlockSpec((None, 1, D_CONV), c2),
            pl.BlockSpec((None, 1, D_CHUNK), c2),
            pl.BlockSpec((None, 1, D_CHUNK), c2),
            pl.BlockSpec((None, N_HEADS, CHUNK, CHUNK), lambda b, t: (l, 0, 0, 0)),
            pl.BlockSpec((None, N_HEADS, CHUNK, CHUNK), lambda b, t: (l, 0, 0, 0)),
        ],
        out_specs=[
            pl.BlockSpec((TT, D), tile),
            pl.BlockSpec((None, POOL_HALO, D_POOL), seq),
            pl.BlockSpec((None, CONV_HALO, D_CONV), seq),
        ],
        out_shape=[
            jax.ShapeDtypeStruct((N_TOK, D), bf16),
            jax.ShapeDtypeStruct((BATCH, POOL_HALO, D_POOL), f32),
            jax.ShapeDtypeStruct((BATCH, CONV_HALO, D_CONV), f32),
        ],
        scratch_shapes=[
            pltpu.VMEM((TT + POOL_HALO, D_POOL), f32),
            pltpu.VMEM((TT + CONV_HALO, D_CONV), f32),
            pltpu.VMEM((TT, D_CONV), f32),
        ],
        compiler_params=_cparams(("arbitrary", "arbitrary")),
        name="mix_prompt",
    )(proj, mix_sample, w_pool, pool_scale.reshape(DEPTH, 1, D_POOL), w_dw, b_dw.reshape(DEPTH, 1, D_CONV),
      clg.reshape(DEPTH, 1, D_CONV), clb.reshape(DEPTH, 1, D_CONV), slg.reshape(DEPTH, 1, D_CHUNK),
      slb.reshape(DEPTH, 1, D_CHUNK), w_spatial, bs_bcast)


def _mixs_kernel(ws_ref, bs_ref, p_ref, hp_ref, hc_ref, wpool_ref, pscale_ref, wdw_ref, bdw_ref,
                 clg_ref, clb_ref, slg_ref, slb_ref, o_ref, glu_ref, v_ref):
    def pool_row(idx, sl):
        if idx < POOL_HIST:
            return hp_ref[idx, :, sl]
        return p_ref[idx - POOL_HIST, :, sl]

    for g, w in enumerate(POOL_WINDOWS):
        sl = slice(g * POOL_GROUP, (g + 1) * POOL_GROUP)
        diffs = []
        for t in range(DEC_SEQ):
            tok = pool_row(POOL_HIST + t, sl)
            s = tok
            for j in range(1, w):
                s = s + pool_row(POOL_HIST + t - j, sl)
            diffs.append(s * (1.0 / w) - tok)
        d = jnp.concatenate(diffs, axis=0).astype(bf16)
        o = jnp.dot(d, wpool_ref[g].astype(bf16), preferred_element_type=f32) * pscale_ref[:, sl]
        for t in range(DEC_SEQ):
            o_ref[t, :, sl] = o[t * SB:(t + 1) * SB].astype(bf16)

    for t in range(DEC_SEQ):
        glu_ref[t] = p_ref[t, :, C1:C2] * _sigmoid(p_ref[t, :, C2:C3])

    def conv_row(idx, cs):
        if idx < CONV_HIST:
            return hc_ref[idx, :, cs]
        return glu_ref[idx - CONV_HIST, :, cs]

    for t in range(DEC_SEQ):
        cols = []
        for c in range(D_CONV // 128):
            cs = slice(c * 128, (c + 1) * 128)
            acc = jnp.zeros((SB, 128), f32)
            for k in range(CONV_WIDTH):
                acc = acc + conv_row(t + k, cs) * wdw_ref[k:k + 1, cs]
            cols.append(acc + bdw_ref[:, cs])
        yn = _layer_norm(jnp.concatenate(cols, axis=-1), clg_ref[...], clb_ref[...])
        o_ref[t, :, C1:C2] = (yn * _sigmoid(yn)).astype(bf16)

    for t in range(DEC_SEQ):
        v_ref[t] = _layer_norm(jax.nn.gelu(p_ref[t, :, C4:P_IN]), slg_ref[...], slb_ref[...])
    for t in range(DEC_SEQ):
        for h in range(N_HEADS):
            hs = slice(h * 128, (h + 1) * 128)
            s = jnp.full((SB, 128), bs_ref[h * DEC_SEQ + t], f32)
            for k in range(t + 1):
                s = s + ws_ref[(h * DEC_SEQ + t) * DEC_SEQ + k] * v_ref[k, :, hs]
            u = jax.nn.gelu(p_ref[t, :, C3 + h * 128:C3 + (h + 1) * 128])
            o_ref[t, :, C2 + h * 128:C2 + (h + 1) * 128] = (u * s).astype(bf16)


def _mixs_call(l, ws_small, bs_small, proj_t, hp_t, hc_t, w_pool, pool_scale, w_dw, b_dw, clg, clb, slg, slb):
    c2 = lambda s: (l, 0, 0)
    return pl.pallas_call(
        _mixs_kernel,
        grid=(DEC_BATCH // SB,),
        in_specs=[
            pl.BlockSpec(memory_space=pltpu.SMEM),
            pl.BlockSpec(memory_space=pltpu.SMEM),
            pl.BlockSpec((DEC_SEQ, SB, P_IN), lambda s: (0, s, 0)),
            pl.BlockSpec((None, POOL_HIST, SB, D_POOL), lambda s: (l, 0, s, 0)),
            pl.BlockSpec((None, CONV_HIST, SB, D_CONV), lambda s: (l, 0, s, 0)),
            pl.BlockSpec((None, 4, POOL_GROUP, POOL_GROUP), lambda s: (l, 0, 0, 0)),
            pl.BlockSpec((None, 1, D_POOL), c2),
            pl.BlockSpec((None, CONV_WIDTH, D_CONV), c2),
            pl.BlockSpec((None, 1, D_CONV), c2),
            pl.BlockSpec((None, 1, D_CONV), c2),
            pl.BlockSpec((None, 1, D_CONV), c2),
            pl.BlockSpec((None, 1, D_CHUNK), c2),
            pl.BlockSpec((None, 1, D_CHUNK), c2),
        ],
        out_specs=[
            pl.BlockSpec((DEC_SEQ, SB, D), lambda s: (0, s, 0)),
            pl.BlockSpec((DEC_SEQ, SB, D_CONV), lambda s: (0, s, 0)),
            pl.BlockSpec((DEC_SEQ, SB, D_CHUNK), lambda s: (0, s, 0)),
        ],
        out_shape=[
            jax.ShapeDtypeStruct((DEC_SEQ, DEC_BATCH, D), bf16),
            jax.ShapeDtypeStruct((DEC_SEQ, DEC_BATCH, D_CONV), f32),
            jax.ShapeDtypeStruct((DEC_SEQ, DEC_BATCH, D_CHUNK), f32),
        ],
        compiler_params=_cparams(("arbitrary",)),
        name="mix_sample",
    )(ws_small, bs_small, proj_t, hp_t, hc_t, w_pool, pool_scale.reshape(DEPTH, 1, D_POOL), w_dw,
      b_dw.reshape(DEPTH, 1, D_CONV), clg.reshape(DEPTH, 1, D_CONV), clb.reshape(DEPTH, 1, D_CONV),
      slg.reshape(DEPTH, 1, D_CHUNK), slb.reshape(DEPTH, 1, D_CHUNK))


def _outproj_kernel(m_ref, w_ref, x_ref, gtp, gts, shp, shs, scp, scs, g_ref, x1_ref, h2_ref, wbf):
    i = pl.program_id(0)

    @pl.when(i == 0)
    def _():
        def cast_rows(r, carry):
            rs = pl.ds(pl.multiple_of(r * 256, 256), 256)
            wbf[rs, :] = w_ref[rs, :].astype(bf16)
            return carry
        lax.fori_loop(0, D // 256, cast_rows, 0)

    mix = jnp.dot(m_ref[...], wbf[...], preferred_element_type=f32)
    x1 = x_ref[...] + _sel_mod(i, TM_OUT, gtp, gts) * mix
    x1_ref[...] = x1
    h2 = _rms(x1, g_ref[...]) * (1.0 + _sel_mod(i, TM_OUT, scp, scs)) + _sel_mod(i, TM_OUT, shp, shs)
    h2_ref[...] = h2.astype(h2_ref.dtype)


def _outproj_call(l, mixcat, w_out, x, modp, mods, g_ffn, h2_dtype):
    gtp, gts = _mod_specs(l, 2, TM_OUT, 1)
    shp, shs = _mod_specs(l, 3, TM_OUT, 1)
    scp, scs = _mod_specs(l, 4, TM_OUT, 1)
    return pl.pallas_call(
        _outproj_kernel,
        grid=(N_TOK // TM_OUT,),
        in_specs=[
            pl.BlockSpec((TM_OUT, D), lambda i: (i, 0)),
            pl.BlockSpec((None, D, D), lambda i: (l, 0, 0), pipeline_mode=pl.Buffered(1)),
            pl.BlockSpec((TM_OUT, D), lambda i: (i, 0)),
            gtp, gts, shp, shs, scp, scs,
            pl.BlockSpec((None, 1, D), lambda i: (l, 0, 0)),
        ],
        out_specs=[
            pl.BlockSpec((TM_OUT, D), lambda i: (i, 0)),
            pl.BlockSpec((TM_OUT, D), lambda i: (i, 0)),
        ],
        out_shape=[
            jax.ShapeDtypeStruct((N_TOK, D), f32),
            jax.ShapeDtypeStruct((N_TOK, D), h2_dtype),
        ],
        scratch_shapes=[pltpu.VMEM((D, D), bf16)],
        compiler_params=_cparams(("arbitrary",)),
        name="out_proj",
    )(mixcat, w_out, x, modp, mods, modp, mods, modp, mods, g_ffn.reshape(DEPTH, 1, D))


def _group_ffn_kernel(ge_ref, gn_ref, rs_ref, tail_ref, x_hbm, wg_ref, wu_ref, wd_ref, y_hbm,
                      xbf, yacc, xstage, wgs, wus, wds, sem, *, sub, nsub, nj, all_full, chunk):
    del ge_ref
    g = pl.program_id(0)
    j = pl.program_id(1)
    n = gn_ref[g]

    @pl.when(jnp.logical_and(g == 0, j == 0))
    def _():
        yacc[0] = jnp.zeros((sub, D), f32)

        def tail_copy(c):
            r0 = pl.multiple_of(tail_ref[0] + c * sub, 8)
            return pltpu.make_async_copy(yacc.at[0], y_hbm.at[pl.ds(r0, sub), :], sem.at[1, 0])

        def start(c, carry):
            tail_copy(c).start()
            return carry

        def wait(c, carry):
            tail_copy(c).wait()
            return carry

        lax.fori_loop(0, tail_ref[1], start, 0)
        lax.fori_loop(0, tail_ref[1], wait, 0)

    def x_copy(m):
        r0 = pl.multiple_of(rs_ref[g * nsub + m], 8)
        return pltpu.make_async_copy(x_hbm.at[pl.ds(r0, sub), :], xstage.at[m % 2], sem.at[0, m % 2])

    def y_copy(grp, m):
        r0 = pl.multiple_of(rs_ref[grp * nsub + m], 8)
        return pltpu.make_async_copy(yacc.at[m], y_hbm.at[pl.ds(r0, sub), :], sem.at[1, m])

    def for_each_subtile(count, fn):
        def body(m, carry):
            fn(m)
            return carry
        lax.fori_loop(0, count, body, 0)

    @pl.when(jnp.logical_and(n > 0, j == 0))
    def _():
        x_copy(0).start()

        def land(m):
            @pl.when(m + 1 < n)
            def _():
                x_copy(m + 1).start()
            x_copy(m).wait()
            xbf[m] = xstage[m % 2].astype(bf16)
        for_each_subtile(n, land)

        @pl.when(g > 0)
        def _():
            for_each_subtile(gn_ref[jnp.maximum(g - 1, 0)], lambda m: y_copy(g - 1, m).wait())

        def clear(m):
            yacc[m] = jnp.zeros((sub, D), f32)
        for_each_subtile(n, clear)

    def swiglu_rows(m0, k, wg, wu, wd):
        x = xbf[pl.ds(m0, k)].reshape(k * sub, D)
        gate = jnp.dot(x, wg, preferred_element_type=f32)
        up = jnp.dot(x, wu, preferred_element_type=f32)
        act = (gate * _sigmoid(gate) * up).astype(bf16)
        yacc[pl.ds(m0, k)] += jnp.dot(act, wd, preferred_element_type=f32).reshape(k, sub, D)

    def narrow_weights_and_first(k):
        wg = wg_ref[...].astype(bf16)
        wu = wu_ref[...].astype(bf16)
        wd = wd_ref[...].astype(bf16)
        wgs[...] = wg
        wus[...] = wu
        wds[...] = wd
        swiglu_rows(0, k, wg, wu, wd)

    if all_full:
        @pl.when(n > 0)
        def _():
            narrow_weights_and_first(1)
            for m in range(1, nsub):
                swiglu_rows(m, 1, wgs[...], wus[...], wds[...])
    else:
        n_wide = n // chunk
        n_tail = n - chunk * n_wide

        @pl.when(n_wide > 0)
        def _():
            narrow_weights_and_first(chunk)

            def wide(t, carry):
                swiglu_rows(t * chunk, chunk, wgs[...], wus[...], wds[...])
                return carry
            lax.fori_loop(1, n_wide, wide, 0)
            for k in range(1, chunk):
                @pl.when(n_tail == k)
                def _():
                    swiglu_rows(n_wide * chunk, k, wgs[...], wus[...], wds[...])

        for k in range(1, chunk):
            @pl.when(jnp.logical_and(n_wide == 0, n_tail == k))
            def _():
                narrow_weights_and_first(k)

    @pl.when(jnp.logical_and(n > 0, j == nj - 1))
    def _():
        for_each_subtile(n, lambda m: y_copy(g, m).start())
        last = g + 1 >= pl.num_programs(0)
        next_n = gn_ref[jnp.minimum(g + 1, pl.num_programs(0) - 1)]

        @pl.when(jnp.logical_or(last, next_n == 0))
        def _():
            for_each_subtile(n, lambda m: y_copy(g, m).wait())


def _group_ffn_call(name, layer, group_expert, group_n, row_start, tail, x, wg, wu, wd, out_rows, sub, nsub,
                    n_groups, all_full, chunk=1):
    d_ff = wg.shape[-1]
    nj = d_ff // TF_FFN

    def jmap(g, j, gn):
        return jnp.where(gn[g] > 0, j, nj - 1)

    return pl.pallas_call(
        functools.partial(_group_ffn_kernel, sub=sub, nsub=nsub, nj=nj, all_full=all_full, chunk=chunk),
        grid_spec=pltpu.PrefetchScalarGridSpec(
            num_scalar_prefetch=4,
            grid=(n_groups, nj),
            in_specs=[
                pl.BlockSpec(memory_space=pl.ANY),
                pl.BlockSpec((None, None, D, TF_FFN), lambda g, j, ge, gn, rs, tl: (layer, ge[g], 0, jmap(g, j, gn))),
                pl.BlockSpec((None, None, D, TF_FFN), lambda g, j, ge, gn, rs, tl: (layer, ge[g], 0, jmap(g, j, gn))),
                pl.BlockSpec((None, None, TF_FFN, D), lambda g, j, ge, gn, rs, tl: (layer, ge[g], jmap(g, j, gn), 0)),
            ],
            out_specs=pl.BlockSpec(memory_space=pl.ANY),
            scratch_shapes=[
                pltpu.VMEM((nsub, sub, D), bf16),
                pltpu.VMEM((nsub, sub, D), f32),
                pltpu.VMEM((2, sub, D), f32),
                pltpu.VMEM((D, TF_FFN), bf16),
                pltpu.VMEM((D, TF_FFN), bf16),
                pltpu.VMEM((TF_FFN, D), bf16),
                pltpu.SemaphoreType.DMA((2, nsub)),
            ],
        ),
        out_shape=jax.ShapeDtypeStruct((out_rows, D), f32),
        compiler_params=_cparams(("arbitrary", "arbitrary")),
        name=name,
    )(group_expert, group_n, row_start, tail, x, wg, wu, wd)


def _dense_ffn_call(l, h2, wg, wu, wd):
    n_groups = N_TOK // (SUB_DENSE * NSUB_DENSE)
    zeros = jnp.zeros((n_groups,), i32)
    return _group_ffn_call(
        "ffn_dense", l // 2, zeros, zeros + NSUB_DENSE, jnp.arange(n_groups * NSUB_DENSE, dtype=i32) * SUB_DENSE,
        jnp.zeros((2,), i32), h2, wg[:, None], wu[:, None], wd[:, None], N_TOK, SUB_DENSE, NSUB_DENSE, n_groups,
        True)


def _residual_out(i, tm, x2, ep, outs, final):
    if final:
        (g_ref,), (yp_ref, ys_ref) = ep, outs
        y = _rms(x2, g_ref[...])
        npt = N_PROMPT // tm

        @pl.when(i < npt)
        def _():
            yp_ref[...] = y

        @pl.when(i >= npt)
        def _():
            ys_ref[...] = y
    else:
        (shp, shs, scp, scs, g_ref), (x_ref, h_ref) = ep, outs
        x_ref[...] = x2
        h_ref[...] = _modulated_norm(i, tm, x2, g_ref, shp, shs, scp, scs)


def _residual_specs(l, tm, nargs, g_mix, g_final):
    if nargs == 1:
        tile = lambda i: (i, 0)
        const3 = lambda i: (l + 1, 0, 0)
        const2 = lambda i: (0, 0)
        pmap = lambda i: (jnp.minimum(i, N_PROMPT // tm - 1), 0)
        smap = lambda i: (jnp.maximum(i - N_PROMPT // tm, 0), 0)
    else:
        tile = lambda i, s: (i, 0)
        const3 = lambda i, s: (l + 1, 0, 0)
        const2 = lambda i, s: (0, 0)
        pmap = lambda i, s: (jnp.minimum(i, N_PROMPT // tm - 1), 0)
        smap = lambda i, s: (jnp.maximum(i - N_PROMPT // tm, 0), 0)
    if l == DEPTH - 1:
        return ([g_final.reshape(1, D)], [pl.BlockSpec((1, D), const2)],
                [pl.BlockSpec((tm, D), pmap), pl.BlockSpec((tm, D), smap)],
                [jax.ShapeDtypeStruct((N_PROMPT, D), f32), jax.ShapeDtypeStruct((N_SAMPLE, D), f32)])
    mod_nargs = 1 if nargs == 1 else 3
    return (None, [*_mod_specs(l + 1, 0, tm, mod_nargs), *_mod_specs(l + 1, 1, tm, mod_nargs),
                   pl.BlockSpec((None, 1, D), const3)],
            [pl.BlockSpec((tm, D), tile), pl.BlockSpec((tm, D), tile)],
            [jax.ShapeDtypeStruct((N_TOK, D), f32), jax.ShapeDtypeStruct((N_TOK, D), bf16)])


def _dense_residual_kernel(y_ref, x1_ref, gtp, gts, *rest, final):
    i = pl.program_id(0)
    n_ep = 1 if final else 5
    x2 = x1_ref[...] + _sel_mod(i, TM, gtp, gts) * y_ref[...]
    _residual_out(i, TM, x2, rest[:n_ep], rest[n_ep:], final)


def _dense_residual_call(l, y, x1, modp, mods, g_mix, g_final):
    final = l == DEPTH - 1
    gtp, gts = _mod_specs(l, 5, TM, 1)
    ep_ops, ep_specs, out_specs, out_shape = _residual_specs(l, TM, 1, g_mix, g_final)
    if ep_ops is None:
        ep_ops = [modp, mods, modp, mods, g_mix.reshape(DEPTH, 1, D)]
    return pl.pallas_call(
        functools.partial(_dense_residual_kernel, final=final),
        grid=(N_TOK // TM,),
        in_specs=[pl.BlockSpec((TM, D), lambda i: (i, 0)), pl.BlockSpec((TM, D), lambda i: (i, 0)), gtp, gts, *ep_specs],
        out_specs=out_specs,
        out_shape=out_shape,
        compiler_params=_cparams(("arbitrary",)),
        name="ffn_residual",
    )(y, x1, modp, mods, *ep_ops)


def _router_kernel(h_ref, wr_ref, route_ref, cnt_ref, carry):
    i = pl.program_id(0)

    @pl.when(i == 0)
    def _():
        carry[...] = jnp.zeros((8, 128), f32)

    h = h_ref[...]
    hh = h.astype(bf16)
    hl = (h - hh.astype(f32)).astype(bf16)
    w = wr_ref[...]
    wh = w.astype(bf16)
    wl = (w - wh.astype(f32)).astype(bf16)
    logits = (jnp.dot(hh, wh, preferred_element_type=f32) + jnp.dot(hl, wh, preferred_element_type=f32)
              + jnp.dot(hh, wl, preferred_element_type=f32))
    lane = lax.broadcasted_iota(i32, (TM, 128), 1)
    neg = jnp.float32(-jnp.inf)
    lg = jnp.where(lane < N_EXPERTS, logits, neg)
    m1 = jnp.max(lg, axis=-1, keepdims=True)
    i1 = jnp.min(jnp.where(lg == m1, lane, 128), axis=-1, keepdims=True)
    lg2 = jnp.where(lane == i1, neg, lg)
    m2 = jnp.max(lg2, axis=-1, keepdims=True)
    i2 = jnp.min(jnp.where(lg2 == m2, lane, 128), axis=-1, keepdims=True)
    e2 = jnp.exp(m2 - m1)
    g1 = 1.0 / (1.0 + e2)
    g2 = e2 / (1.0 + e2)

    hit1 = lane == i1
    hit2 = lane == i2
    cnt = (hit1 | hit2).astype(f32)
    before = (lax.broadcasted_iota(i32, (TM, TM), 0) > lax.broadcasted_iota(i32, (TM, TM), 1))
    prefix = jnp.dot(before.astype(bf16), cnt.astype(bf16), preferred_element_type=f32) + carry[0:1, :]
    r1 = jnp.sum(jnp.where(hit1, prefix, 0.0), axis=-1, keepdims=True)
    r2 = jnp.sum(jnp.where(hit2, prefix, 0.0), axis=-1, keepdims=True)
    total = carry[...] + jnp.sum(cnt, axis=0, keepdims=True)
    carry[...] = total
    cnt_ref[...] = total

    out = jnp.where(lane == 0, i1.astype(f32), 0.0)
    out = jnp.where(lane == 1, i2.astype(f32), out)
    out = jnp.where(lane == 2, g1, out)
    out = jnp.where(lane == 3, g2, out)
    out = jnp.where(lane == 4, r1, out)
    out = jnp.where(lane == 5, r2, out)
    route_ref[...] = out


def _router_call(h2, wr_pad):
    return pl.pallas_call(
        _router_kernel,
        grid=(N_TOK // TM,),
        in_specs=[
            pl.BlockSpec((TM, D), lambda i: (i, 0)),
            pl.BlockSpec((D, 128), lambda i: (0, 0)),
        ],
        out_specs=[
            pl.BlockSpec((TM, 128), lambda i: (i, 0)),
            pl.BlockSpec((8, 128), lambda i: (0, 0)),
        ],
        out_shape=[
            jax.ShapeDtypeStruct((N_TOK, 128), f32),
            jax.ShapeDtypeStruct((8, 128), f32),
        ],
        scratch_shapes=[pltpu.VMEM((8, 128), f32)],
        compiler_params=_cparams(("arbitrary",)),
        name="moe_router",
    )(h2, wr_pad)


def _dispatch_kernel(pos_ref, pad_ref, h_ref, xs_ref, zbuf, sem, zsem, tsem):
    i = pl.program_id(0)

    def zero_copy(r):
        return pltpu.make_async_copy(zbuf.at[pl.ds(0, 1), :], xs_ref.at[pl.ds(r, 1), :], zsem)

    def tail_copy(c):
        r0 = pl.multiple_of(pad_ref[2 * N_EXPERTS] + c * SUB_MOE, 8)
        return pltpu.make_async_copy(zbuf, xs_ref.at[pl.ds(r0, SUB_MOE), :], tsem)

    def for_each_pad_row(fn):
        for e in range(N_EXPERTS):
            def body(r, carry):
                fn(r)
                return carry
            lax.fori_loop(pad_ref[e], pad_ref[N_EXPERTS + e], body, 0)

    def for_each_tail_tile(fn):
        def body(c, carry):
            fn(c)
            return carry
        lax.fori_loop(0, pad_ref[2 * N_EXPERTS + 1], body, 0)

    @pl.when(i == 0)
    def _():
        zbuf[...] = jnp.zeros((SUB_MOE, D), f32)
        for_each_pad_row(lambda r: zero_copy(r).start())
        for_each_tail_tile(lambda c: tail_copy(c).start())

    def row_copy(r, slot):
        p = pos_ref[2 * (i * TM + r) + slot]
        return pltpu.make_async_copy(h_ref.at[pl.ds(r, 1), :], xs_ref.at[pl.ds(p, 1), :], sem)

    def issue(r, carry):
        row_copy(r, 0).start(priority=0)
        row_copy(r, 1).start(priority=1)
        return carry

    lax.fori_loop(0, TM, issue, 0, unroll=8)
    for _ in range(2):
        pltpu.make_async_copy(h_ref, xs_ref.at[pl.ds(0, TM), :], sem).wait()

    @pl.when(i == 0)
    def _():
        for_each_pad_row(lambda r: zero_copy(r).wait())
        for_each_tail_tile(lambda c: tail_copy(c).wait())


def _dispatch_call(pos_flat, pad_bounds, h2):
    return pl.pallas_call(
        _dispatch_kernel,
        grid_spec=pltpu.PrefetchScalarGridSpec(
            num_scalar_prefetch=2,
            grid=(N_TOK // TM,),
            in_specs=[pl.BlockSpec((TM, D), lambda i, pos, pad: (i, 0))],
            out_specs=pl.BlockSpec(memory_space=pl.ANY),
            scratch_shapes=[pltpu.VMEM((SUB_MOE, D), f32), pltpu.SemaphoreType.DMA(()),
                            pltpu.SemaphoreType.DMA(()), pltpu.SemaphoreType.DMA(())],
        ),
        out_shape=jax.ShapeDtypeStruct((P_ROWS, D), f32),
        compiler_params=_cparams(("arbitrary",)),
        name="moe_dispatch",
    )(pos_flat, pad_bounds, h2)


def _combine_kernel(pos_ref, y_ref, x1_ref, route_ref, gtp, gts, *rest, final):
    n_ep = 1 if final else 5
    ep, outs, (buf, sem) = rest[:n_ep], rest[n_ep:-2], rest[-2:]
    i = pl.program_id(0)
    nt = pl.num_programs(0)
    slot = i % 2

    def start_tile(t, s):
        def issue(r, carry):
            for k in range(2):
                p = pos_ref[2 * (t * TM_COMB + r) + k]
                pltpu.make_async_copy(y_ref.at[pl.ds(p, 1), :], buf.at[s, k, pl.ds(r, 1), :],
                                      sem.at[s]).start(priority=k)
            return carry
        lax.fori_loop(0, TM_COMB, issue, 0, unroll=8)

    @pl.when(i == 0)
    def _():
        start_tile(0, 0)

    @pl.when(i + 1 < nt)
    def _():
        start_tile(i + 1, 1 - slot)

    for k in range(2):
        pltpu.make_async_copy(y_ref.at[pl.ds(0, TM_COMB), :], buf.at[slot, k], sem.at[slot]).wait()
    f = route_ref[:, 2:3] * buf[slot, 0] + route_ref[:, 3:4] * buf[slot, 1]
    x2 = x1_ref[...] + _sel_mod(i, TM_COMB, gtp, gts) * f
    _residual_out(i, TM_COMB, x2, ep, outs, final)


def _combine_call(l, pos_flat, y, x1, route, modp, mods, g_mix, g_final):
    final = l == DEPTH - 1
    gtp, gts = _mod_specs(l, 5, TM_COMB, 3)
    ep_ops, ep_specs, out_specs, out_shape = _residual_specs(l, TM_COMB, 2, g_mix, g_final)
    if ep_ops is None:
        ep_ops = [modp, mods, modp, mods, g_mix.reshape(DEPTH, 1, D)]
    return pl.pallas_call(
        functools.partial(_combine_kernel, final=final),
        grid_spec=pltpu.PrefetchScalarGridSpec(
            num_scalar_prefetch=1,
            grid=(N_TOK // TM_COMB,),
            in_specs=[
                pl.BlockSpec(memory_space=pl.ANY),
                pl.BlockSpec((TM_COMB, D), lambda i, pos: (i, 0)),
                pl.BlockSpec((TM_COMB, 128), lambda i, pos: (i, 0)),
                gtp, gts, *ep_specs,
            ],
            out_specs=out_specs,
            scratch_shapes=[
                pltpu.VMEM((2, 2, TM_COMB, D), f32),
                pltpu.SemaphoreType.DMA((2,)),
            ],
        ),
        out_shape=out_shape,
        compiler_params=_cparams(("arbitrary",)),
        name="moe_combine",
    )(pos_flat, y, x1, route, modp, mods, *ep_ops)


def _moe_layer(l, h2, x1, w_router, wg, wu, wd, modp, mods, g_mix, g_final):
    lm = l // 2
    wr_pad = jnp.pad(w_router[lm], ((0, 0), (0, 128 - N_EXPERTS)))
    route, cnt = _router_call(h2, wr_pad)
    expert = route[:, 0:2].astype(i32)
    rank = route[:, 4:6].astype(i32)
    counts = cnt[0, :N_EXPERTS].astype(i32)
    tiles = (counts + SUB_MOE - 1) // SUB_MOE
    tile_start = jnp.cumsum(tiles) - tiles
    row_start = tile_start * SUB_MOE
    pos_flat = (row_start[expert] + rank).reshape(-1)
    n_tiles = jnp.sum(tiles)
    tail = jnp.stack([n_tiles * SUB_MOE, MAX_TILES - n_tiles])
    pad_bounds = jnp.concatenate([row_start + counts, row_start + tiles * SUB_MOE, tail])
    groups = (tiles + NSUB_MOE - 1) // NSUB_MOE
    group_end = jnp.cumsum(groups)
    n_groups = group_end[-1]
    gid = jnp.arange(MAX_GROUPS, dtype=i32)
    gcl = jnp.minimum(gid, n_groups - 1)
    group_expert = jnp.sum((gcl[:, None] >= group_end[None, :]).astype(i32), axis=1)
    per_group = tiles // jnp.maximum(groups, 1)
    extra = tiles - per_group * groups
    k = gcl - (group_end - groups)[group_expert]
    first = k * per_group[group_expert] + jnp.minimum(k, extra[group_expert])
    group_n = jnp.where(gid < n_groups, per_group[group_expert] + (k < extra[group_expert]).astype(i32), 0)
    sub_start = (tile_start[group_expert] + first)[:, None] + jnp.arange(NSUB_MOE, dtype=i32)[None, :]
    xs = _dispatch_call(pos_flat, pad_bounds, h2)
    y = _group_ffn_call("moe_experts", lm, group_expert, group_n, (sub_start * SUB_MOE).reshape(-1), tail, xs,
                        wg, wu, wd, P_ROWS, SUB_MOE, NSUB_MOE, MAX_GROUPS, False, CHUNK_MOE)
    return _combine_call(l, pos_flat, y, x1, route, modp, mods, g_mix, g_final)


def kernel(x_prompt, x_sample, state_pool, state_conv, c_prompt, c_sample, w_ada, b_ada, g_mix, g_ffn, w_in, w_pool, pool_scale, w_dw, b_dw, conv_ln_g, conv_ln_b, sgu_ln_g, sgu_ln_b, w_spatial, b_spatial, w_out, w_ff_gate, w_ff_up, w_ff_down, w_router, w_exp_gate, w_exp_up, w_exp_down, g_final):
    c_all = jnp.concatenate([jnp.repeat(c_sample, DEC_SEQ, axis=0), c_prompt,
                             jnp.zeros((N_SEQ_PAD - N_SAMPLE - BATCH, D), f32)], axis=0)
    modp = mods = _ada_call(c_all, w_ada, b_ada)
    hp_t = state_pool.transpose(0, 2, 1, 3)
    hc_t = state_conv.transpose(0, 2, 1, 3)
    bs_bcast = jnp.broadcast_to(b_spatial[:, :, :, None], (DEPTH, N_HEADS, CHUNK, CHUNK))
    ws_small = w_spatial[:, :, :DEC_SEQ, :DEC_SEQ].reshape(DEPTH, -1)
    bs_small = b_spatial[:, :, :DEC_SEQ].reshape(DEPTH, -1)
    w_in_bf = w_in.astype(bf16)

    x, h1 = _prenorm_call(x_prompt, x_sample, modp, mods, g_mix)
    pool_p, conv_p, pool_s, conv_s, chunk_v = [], [], [], [], []
    for l in range(DEPTH):
        proj = _inproj_call(l, h1, w_in_bf)
        proj_s = proj[N_PROMPT:].reshape(DEC_BATCH, DEC_SEQ, P_IN)
        mix_s, glu_s, v_s = _mixs_call(l, ws_small[l], bs_small[l], proj_s.transpose(1, 0, 2), hp_t, hc_t,
                                       w_pool, pool_scale, w_dw, b_dw, conv_ln_g, conv_ln_b, sgu_ln_g, sgu_ln_b)
        mixcat, npool, nconv = _mixp_call(l, proj, mix_s.transpose(1, 0, 2).reshape(N_SAMPLE, D), w_pool, pool_scale,
                                          w_dw, b_dw, conv_ln_g, conv_ln_b, sgu_ln_g, sgu_ln_b, w_spatial, bs_bcast)
        pool_p.append(npool[:, POOL_HALO - POOL_HIST:])
        conv_p.append(nconv[:, CONV_HALO - CONV_HIST:])
        pool_s.append(jnp.concatenate([state_pool[l][:, DEC_SEQ:], proj_s[:, :, :D_POOL]], axis=1))
        conv_s.append(jnp.concatenate([state_conv[l][:, DEC_SEQ:], glu_s.transpose(1, 0, 2)], axis=1))
        chunk_v.append(v_s.transpose(1, 0, 2))
        x1, h2 = _outproj_call(l, mixcat, w_out, x, modp, mods, g_ffn, f32)
        if l % 2 == 0:
            y = _dense_ffn_call(l, h2, w_ff_gate, w_ff_up, w_ff_down)
            out = _dense_residual_call(l, y, x1, modp, mods, g_mix, g_final)
        else:
            out = _moe_layer(l, h2, x1, w_router, w_exp_gate, w_exp_up, w_exp_down, modp, mods, g_mix, g_final)
        if l < DEPTH - 1:
            x, h1 = out
    y_prompt, y_sample = out
    return (y_prompt.reshape(BATCH, SEQ, D), y_sample.reshape(DEC_BATCH, DEC_SEQ, D),
            jnp.stack(pool_p), jnp.stack(conv_p), jnp.stack(pool_s), jnp.stack(conv_s), jnp.stack(chunk_v))
```

```python
import functools

import jax
import jax.numpy as jnp
from jax import lax
from jax.experimental import pallas as pl
from jax.experimental.pallas import tpu as pltpu

f32 = jnp.float32
bf16 = jnp.bfloat16
i32 = jnp.int32

D = 2048
BATCH = 4
SEQ = 2048
DEPTH = 4
DEC_BATCH = 128
DEC_SEQ = 4
PAST_LEN = 16384
D_POOL = 512
POOL_WINDOWS = (2, 4, 8, 16)
POOL_GROUP = 128
POOL_HIST = 15
D_CONV = 768
CONV_WIDTH = 31
CONV_HIST = 30
D_CHUNK = 768
CHUNK = 128
N_HEADS = 6
P_IN = 3584
D_FF = 5632
N_EXPERTS = 8
D_FF_EXPERT = 7168
EPS = 1e-6

N_PROMPT = BATCH * SEQ
N_SAMPLE = DEC_BATCH * DEC_SEQ
N_TOK = N_PROMPT + N_SAMPLE
N_SEQ = BATCH + DEC_BATCH
N_SEQ_PAD = N_SAMPLE + 8

C1 = D_POOL
C2 = C1 + D_CONV
C3 = C2 + D_CONV
C4 = C3 + D_CHUNK

VMEM_LIMIT = 56 * 1024 * 1024
TM = 512
TM_OUT = 256
TN_ADA = 1024
TT = 256
ROW_CHUNK = 64
POOL_HALO = 16
CONV_HALO = 32
SB = 32
TF_FFN = 256
SUB_MOE = 128
NSUB_MOE = 22
CHUNK_MOE = 6
SUB_DENSE = N_TOK // 16
NSUB_DENSE = 4
TM_COMB = 256
MAX_TILES = (N_TOK * 2) // SUB_MOE + N_EXPERTS
MAX_GROUPS = (MAX_TILES + (NSUB_MOE - 1) * N_EXPERTS) // NSUB_MOE
P_ROWS = MAX_TILES * SUB_MOE


def _cparams(sem):
    return pltpu.CompilerParams(dimension_semantics=sem, vmem_limit_bytes=VMEM_LIMIT)


def _sigmoid(x):
    return jax.nn.sigmoid(x)


def _rms(x, g):
    return x * lax.rsqrt(jnp.mean(x * x, axis=-1, keepdims=True) + EPS) * g


def _layer_norm(y, g, b):
    mu = jnp.mean(y, axis=-1, keepdims=True)
    d = y - mu
    var = jnp.mean(d * d, axis=-1, keepdims=True)
    return d * lax.rsqrt(var + EPS) * g + b


def _sel_mod(i, tm, p_ref, s_ref):
    seq = jnp.minimum(i // (SEQ // tm), BATCH - 1)
    return jnp.where(i >= N_PROMPT // tm, s_ref[...], p_ref[pl.ds(seq, 1), :])


def _mod_specs(l, chunk, tm, nargs):
    npt = N_PROMPT // tm
    pblk = N_SAMPLE // 8
    if nargs == 1:
        mp = lambda i: (l, pblk, chunk)
        ms = lambda i: (l, jnp.maximum(i - npt, 0), chunk)
    elif nargs == 2:
        mp = lambda i, j: (l, pblk, chunk)
        ms = lambda i, j: (l, jnp.maximum(i - npt, 0), chunk)
    else:
        mp = lambda i, s: (l, pblk, chunk)
        ms = lambda i, s: (l, jnp.maximum(i - npt, 0), chunk)
    return pl.BlockSpec((None, 8, D), mp), pl.BlockSpec((None, tm, D), ms, pipeline_mode=pl.Buffered(1))


def _ada_kernel(c_ref, w_ref, b_ref, o_ref):
    c = c_ref[...]
    s = (c * _sigmoid(c)).astype(bf16)
    o_ref[...] = jnp.dot(s, w_ref[...].astype(bf16), preferred_element_type=f32) + b_ref[...]


def _ada_call(c_all, w_ada, b_ada):
    nj = 6 * D // TN_ADA
    return pl.pallas_call(
        _ada_kernel,
        grid=(DEPTH, nj),
        in_specs=[
            pl.BlockSpec((N_SEQ_PAD, D), lambda l, j: (0, 0)),
            pl.BlockSpec((None, D, TN_ADA), lambda l, j: (l, 0, j)),
            pl.BlockSpec((None, 1, TN_ADA), lambda l, j: (l, 0, j)),
        ],
        out_specs=pl.BlockSpec((None, N_SEQ_PAD, TN_ADA), lambda l, j: (l, 0, j)),
        out_shape=jax.ShapeDtypeStruct((DEPTH, N_SEQ_PAD, 6 * D), f32),
        compiler_params=_cparams(("arbitrary", "arbitrary")),
        name="ada_mod",
    )(c_all, w_ada, b_ada.reshape(DEPTH, 1, 6 * D))


def _modulated_norm(i, tm, x, g_ref, shp, shs, scp, scs):
    h = _rms(x, g_ref[...]) * (1.0 + _sel_mod(i, tm, scp, scs)) + _sel_mod(i, tm, shp, shs)
    return h.astype(bf16)


def _prenorm_kernel(xp_ref, xs_ref, shp, shs, scp, scs, g_ref, x_ref, h_ref):
    i = pl.program_id(0)
    x = jnp.where(i >= N_PROMPT // TM, xs_ref[...], xp_ref[...])
    x_ref[...] = x
    h_ref[...] = _modulated_norm(i, TM, x, g_ref, shp, shs, scp, scs)


def _prenorm_call(x_prompt, x_sample, modp, mods, g_mix):
    assert N_SAMPLE == TM
    shp, shs = _mod_specs(0, 0, TM, 1)
    scp, scs = _mod_specs(0, 1, TM, 1)
    npt = N_PROMPT // TM
    return pl.pallas_call(
        _prenorm_kernel,
        grid=(N_TOK // TM,),
        in_specs=[
            pl.BlockSpec((TM, D), lambda i: (jnp.minimum(i, npt - 1), 0)),
            pl.BlockSpec((TM, D), lambda i: (0, 0), pipeline_mode=pl.Buffered(1)),
            shp, shs, scp, scs,
            pl.BlockSpec((None, 1, D), lambda i: (0, 0, 0)),
        ],
        out_specs=[pl.BlockSpec((TM, D), lambda i: (i, 0)), pl.BlockSpec((TM, D), lambda i: (i, 0))],
        out_shape=[jax.ShapeDtypeStruct((N_TOK, D), f32), jax.ShapeDtypeStruct((N_TOK, D), bf16)],
        compiler_params=_cparams(("arbitrary",)),
        name="prenorm",
    )(x_prompt.reshape(N_PROMPT, D), x_sample.reshape(N_SAMPLE, D), modp, mods, modp, mods,
      g_mix.reshape(DEPTH, 1, D))


def _inproj_kernel(h_ref, w_ref, o_ref):
    o_ref[...] = jnp.dot(h_ref[...], w_ref[...], preferred_element_type=f32)


def _inproj_call(l, h1, w_in_bf):
    return pl.pallas_call(
        _inproj_kernel,
        grid=(N_TOK // TM,),
        in_specs=[
            pl.BlockSpec((TM, D), lambda i: (i, 0)),
            pl.BlockSpec((None, D, P_IN), lambda i: (l, 0, 0), pipeline_mode=pl.Buffered(1)),
        ],
        out_specs=pl.BlockSpec((TM, P_IN), lambda i: (i, 0)),
        out_shape=jax.ShapeDtypeStruct((N_TOK, P_IN), f32),
        compiler_params=_cparams(("arbitrary",)),
        name="in_proj",
    )(h1, w_in_bf)


def _mixp_kernel(p_ref, ms_ref, wpool_ref, pscale_ref, wdw_ref, bdw_ref, clg_ref, clb_ref, slg_ref, slb_ref,
                 ws_ref, bs_ref, o_ref, *state_and_scratch):
    b = pl.program_id(0)

    @pl.when(b < BATCH)
    def _():
        _mixp_body(p_ref, wpool_ref, pscale_ref, wdw_ref, bdw_ref, clg_ref, clb_ref, slg_ref, slb_ref,
                   ws_ref, bs_ref, o_ref, *state_and_scratch)

    @pl.when(jnp.logical_and(b == BATCH, pl.program_id(1) < N_SAMPLE // TT))
    def _():
        o_ref[...] = ms_ref[...]


def _mixp_body(p_ref, wpool_ref, pscale_ref, wdw_ref, bdw_ref, clg_ref, clb_ref, slg_ref, slb_ref,
               ws_ref, bs_ref, o_ref, npool_ref, nconv_ref, pext, cext, ybuf):
    t = pl.program_id(1)

    @pl.when(t == 0)
    def _():
        pext[0:POOL_HALO, :] = jnp.zeros((POOL_HALO, D_POOL), f32)
        cext[0:CONV_HALO, :] = jnp.zeros((CONV_HALO, D_CONV), f32)

    pext[POOL_HALO:POOL_HALO + TT, :] = p_ref[:, 0:C1]
    pos = lax.broadcasted_iota(i32, (TT, POOL_GROUP), 0) + t * TT
    for g, w in enumerate(POOL_WINDOWS):
        sl = slice(g * POOL_GROUP, (g + 1) * POOL_GROUP)
        tok = pext[POOL_HALO:POOL_HALO + TT, sl]
        s = tok
        for j in range(1, w):
            s = s + pext[POOL_HALO - j:POOL_HALO - j + TT, sl]
        cnt = jnp.minimum(pos + 1, w).astype(f32)
        diff = (s / cnt - tok).astype(bf16)
        o = jnp.dot(diff, wpool_ref[g].astype(bf16), preferred_element_type=f32) * pscale_ref[:, sl]
        o_ref[:, sl] = o.astype(bf16)
    tail = pext[TT:TT + POOL_HALO, :]
    npool_ref[...] = tail
    pext[0:POOL_HALO, :] = tail

    cext[CONV_HALO:CONV_HALO + TT, :] = p_ref[:, C1:C2] * _sigmoid(p_ref[:, C2:C3])
    off = CONV_HALO - CONV_HIST
    for c in range(D_CONV // 128):
        cs = slice(c * 128, (c + 1) * 128)
        for r in range(TT // ROW_CHUNK):
            r0 = r * ROW_CHUNK
            acc = bdw_ref[:, cs]
            for res in range(8):
                rows = ROW_CHUNK if res == 0 else ROW_CHUNK + 8
                z = None
                for a in range((CONV_WIDTH + off) // 8 + 1):
                    k = 8 * a + res - off
                    if 0 <= k < CONV_WIDTH:
                        term = cext[r0 + 8 * a:r0 + 8 * a + rows, cs] * wdw_ref[k:k + 1, cs]
                        z = term if z is None else z + term
                acc = acc + (z if res == 0 else pltpu.roll(z, rows - res, axis=0)[0:ROW_CHUNK])
            ybuf[r0:r0 + ROW_CHUNK, cs] = acc
    yn = _layer_norm(ybuf[...], clg_ref[...], clb_ref[...])
    o_ref[:, C1:C2] = (yn * _sigmoid(yn)).astype(bf16)
    ctail = cext[TT:TT + CONV_HALO, :]
    nconv_ref[...] = ctail
    cext[0:CONV_HALO, :] = ctail

    ybuf[...] = _layer_norm(jax.nn.gelu(p_ref[:, C4:P_IN]), slg_ref[...], slb_ref[...])
    causal = (lax.broadcasted_iota(i32, (CHUNK, CHUNK), 0) >= lax.broadcasted_iota(i32, (CHUNK, CHUNK), 1))
    for h in range(N_HEADS):
        hs = slice(h * 128, (h + 1) * 128)
        wm = jnp.where(causal, ws_ref[h], 0.0).astype(bf16)
        for n in range(TT // CHUNK):
            rs = slice(n * CHUNK, (n + 1) * CHUNK)
            s = jnp.dot(wm, ybuf[rs, hs].astype(bf16), preferred_element_type=f32) + bs_ref[h]
            u = jax.nn.gelu(p_ref[rs, C3 + h * 128:C3 + (h + 1) * 128])
            o_ref[rs, C2 + h * 128:C2 + (h + 1) * 128] = (u * s).astype(bf16)


def _mixp_call(l, proj, mix_sample, w_pool, pool_scale, w_dw, b_dw, clg, clb, slg, slb, w_spatial, bs_bcast):
    nt = SEQ // TT
    last = N_TOK // TT - 1
    c2 = lambda b, t: (l, 0, 0)
    tile = lambda b, t: (jnp.minimum(b * nt + t, last), 0)
    seq = lambda b, t: (jnp.minimum(b, BATCH - 1), 0, 0)
    return pl.pallas_call(
        _mixp_kernel,
        grid=(BATCH + 1, nt),
        in_specs=[
            pl.BlockSpec((TT, P_IN), tile),
            pl.BlockSpec((TT, D), lambda b, t: (jnp.where(b == BATCH, jnp.minimum(t, N_SAMPLE // TT - 1), 0), 0)),
            pl.BlockSpec((None, 4, POOL_GROUP, POOL_GROUP), lambda b, t: (l, 0, 0, 0)),
            pl.BlockSpec((None, 1, D_POOL), c2),
            pl.BlockSpec((None, CONV_WIDTH, D_CONV), c2),
            pl.BlockSpec((None, 1, D_CONV), c2),
            pl.BlockSpec((None, 1, D_CONV), c2),
            pl.BlockSpec((None, 1, D_CONV), c2),
            pl.BlockSpec((None, 1, D_CHUNK), c2),
            pl.BlockSpec((None, 1, D_CHUNK), c2),
            pl.BlockSpec((None, N_HEADS, CHUNK, CHUNK), lambda b, t: (l, 0, 0, 0)),
            pl.BlockSpec((None, N_HEADS, CHUNK, CHUNK), lambda b, t: (l, 0, 0, 0)),
        ],
        out_specs=[
            pl.BlockSpec((TT, D), tile),
            pl.BlockSpec((None, POOL_HALO, D_POOL), seq),
            pl.BlockSpec((None, CONV_HALO, D_CONV), seq),
        ],
        out_shape=[
            jax.ShapeDtypeStruct((N_TOK, D), bf16),
            jax.ShapeDtypeStruct((BATCH, POOL_HALO, D_POOL), f32),
            jax.ShapeDtypeStruct((BATCH, CONV_HALO, D_CONV), f32),
        ],
        scratch_shapes=[
            pltpu.VMEM((TT + POOL_HALO, D_POOL), f32),
            pltpu.VMEM((TT + CONV_HALO, D_CONV), f32),
            pltpu.VMEM((TT, D_CONV), f32),
        ],
        compiler_params=_cparams(("arbitrary", "arbitrary")),
        name="mix_prompt",
    )(proj, mix_sample, w_pool, pool_scale.reshape(DEPTH, 1, D_POOL), w_dw, b_dw.reshape(DEPTH, 1, D_CONV),
      clg.reshape(DEPTH, 1, D_CONV), clb.reshape(DEPTH, 1, D_CONV), slg.reshape(DEPTH, 1, D_CHUNK),
      slb.reshape(DEPTH, 1, D_CHUNK), w_spatial, bs_bcast)


def _mixs_kernel(ws_ref, bs_ref, p_ref, hp_ref, hc_ref, wpool_ref, pscale_ref, wdw_ref, bdw_ref,
                 clg_ref, clb_ref, slg_ref, slb_ref, o_ref, glu_ref, v_ref):
    def pool_row(idx, sl):
        if idx < POOL_HIST:
            return hp_ref[idx, :, sl]
        return p_ref[idx - POOL_HIST, :, sl]

    for g, w in enumerate(POOL_WINDOWS):
        sl = slice(g * POOL_GROUP, (g + 1) * POOL_GROUP)
        diffs = []
        for t in range(DEC_SEQ):
            tok = pool_row(POOL_HIST + t, sl)
            s = tok
            for j in range(1, w):
                s = s + pool_row(POOL_HIST + t - j, sl)
            diffs.append(s * (1.0 / w) - tok)
        d = jnp.concatenate(diffs, axis=0).astype(bf16)
        o = jnp.dot(d, wpool_ref[g].astype(bf16), preferred_element_type=f32) * pscale_ref[:, sl]
        for t in range(DEC_SEQ):
            o_ref[t, :, sl] = o[t * SB:(t + 1) * SB].astype(bf16)

    for t in range(DEC_SEQ):
        glu_ref[t] = p_ref[t, :, C1:C2] * _sigmoid(p_ref[t, :, C2:C3])

    def conv_row(idx, cs):
        if idx < CONV_HIST:
            return hc_ref[idx, :, cs]
        return glu_ref[idx - CONV_HIST, :, cs]

    for t in range(DEC_SEQ):
        cols = []
        for c in range(D_CONV // 128):
            cs = slice(c * 128, (c + 1) * 128)
            acc = jnp.zeros((SB, 128), f32)
            for k in range(CONV_WIDTH):
                acc = acc + conv_row(t + k, cs) * wdw_ref[k:k + 1, cs]
            cols.append(acc + bdw_ref[:, cs])
        yn = _layer_norm(jnp.concatenate(cols, axis=-1), clg_ref[...], clb_ref[...])
        o_ref[t, :, C1:C2] = (yn * _sigmoid(yn)).astype(bf16)

    for t in range(DEC_SEQ):
        v_ref[t] = _layer_norm(jax.nn.gelu(p_ref[t, :, C4:P_IN]), slg_ref[...], slb_ref[...])
    for t in range(DEC_SEQ):
        for h in range(N_HEADS):
            hs = slice(h * 128, (h + 1) * 128)
            s = jnp.full((SB, 128), bs_ref[h * DEC_SEQ + t], f32)
            for k in range(t + 1):
                s = s + ws_ref[(h * DEC_SEQ + t) * DEC_SEQ + k] * v_ref[k, :, hs]
            u = jax.nn.gelu(p_ref[t, :, C3 + h * 128:C3 + (h + 1) * 128])
            o_ref[t, :, C2 + h * 128:C2 + (h + 1) * 128] = (u * s).astype(bf16)


def _mixs_call(l, ws_small, bs_small, proj_t, hp_t, hc_t, w_pool, pool_scale, w_dw, b_dw, clg, clb, slg, slb):
    c2 = lambda s: (l, 0, 0)
    return pl.pallas_call(
        _mixs_kernel,
        grid=(DEC_BATCH // SB,),
        in_specs=[
            pl.BlockSpec(memory_space=pltpu.SMEM),
            pl.BlockSpec(memory_space=pltpu.SMEM),
            pl.BlockSpec((DEC_SEQ, SB, P_IN), lambda s: (0, s, 0)),
            pl.BlockSpec((None, POOL_HIST, SB, D_POOL), lambda s: (l, 0, s, 0)),
            pl.BlockSpec((None, CONV_HIST, SB, D_CONV), lambda s: (l, 0, s, 0)),
            pl.BlockSpec((None, 4, POOL_GROUP, POOL_GROUP), lambda s: (l, 0, 0, 0)),
            pl.BlockSpec((None, 1, D_POOL), c2),
            pl.BlockSpec((None, CONV_WIDTH, D_CONV), c2),
            pl.BlockSpec((None, 1, D_CONV), c2),
            pl.BlockSpec((None, 1, D_CONV), c2),
            pl.BlockSpec((None, 1, D_CONV), c2),
            pl.BlockSpec((None, 1, D_CHUNK), c2),
            pl.BlockSpec((None, 1, D_CHUNK), c2),
        ],
        out_specs=[
            pl.BlockSpec((DEC_SEQ, SB, D), lambda s: (0, s, 0)),
            pl.BlockSpec((DEC_SEQ, SB, D_CONV), lambda s: (0, s, 0)),
            pl.BlockSpec((DEC_SEQ, SB, D_CHUNK), lambda s: (0, s, 0)),
        ],
        out_shape=[
            jax.ShapeDtypeStruct((DEC_SEQ, DEC_BATCH, D), bf16),
            jax.ShapeDtypeStruct((DEC_SEQ, DEC_BATCH, D_CONV), f32),
            jax.ShapeDtypeStruct((DEC_SEQ, DEC_BATCH, D_CHUNK), f32),
        ],
        compiler_params=_cparams(("arbitrary",)),
        name="mix_sample",
    )(ws_small, bs_small, proj_t, hp_t, hc_t, w_pool, pool_scale.reshape(DEPTH, 1, D_POOL), w_dw,
      b_dw.reshape(DEPTH, 1, D_CONV), clg.reshape(DEPTH, 1, D_CONV), clb.reshape(DEPTH, 1, D_CONV),
      slg.reshape(DEPTH, 1, D_CHUNK), slb.reshape(DEPTH, 1, D_CHUNK))


def _outproj_kernel(m_ref, w_ref, x_ref, gtp, gts, shp, shs, scp, scs, g_ref, x1_ref, h2_ref, wbf):
    i = pl.program_id(0)

    @pl.when(i == 0)
    def _():
        def cast_rows(r, carry):
            rs = pl.ds(pl.multiple_of(r * 256, 256), 256)
            wbf[rs, :] = w_ref[rs, :].astype(bf16)
            return carry
        lax.fori_loop(0, D // 256, cast_rows, 0)

    mix = jnp.dot(m_ref[...], wbf[...], preferred_element_type=f32)
    x1 = x_ref[...] + _sel_mod(i, TM_OUT, gtp, gts) * mix
    x1_ref[...] = x1
    h2 = _rms(x1, g_ref[...]) * (1.0 + _sel_mod(i, TM_OUT, scp, scs)) + _sel_mod(i, TM_OUT, shp, shs)
    h2_ref[...] = h2.astype(h2_ref.dtype)


def _outproj_call(l, mixcat, w_out, x, modp, mods, g_ffn, h2_dtype):
    gtp, gts = _mod_specs(l, 2, TM_OUT, 1)
    shp, shs = _mod_specs(l, 3, TM_OUT, 1)
    scp, scs = _mod_specs(l, 4, TM_OUT, 1)
    return pl.pallas_call(
        _outproj_kernel,
        grid=(N_TOK // TM_OUT,),
        in_specs=[
            pl.BlockSpec((TM_OUT, D), lambda i: (i, 0)),
            pl.BlockSpec((None, D, D), lambda i: (l, 0, 0), pipeline_mode=pl.Buffered(1)),
            pl.BlockSpec((TM_OUT, D), lambda i: (i, 0)),
            gtp, gts, shp, shs, scp, scs,
            pl.BlockSpec((None, 1, D), lambda i: (l, 0, 0)),
        ],
        out_specs=[
            pl.BlockSpec((TM_OUT, D), lambda i: (i, 0)),
            pl.BlockSpec((TM_OUT, D), lambda i: (i, 0)),
        ],
        out_shape=[
            jax.ShapeDtypeStruct((N_TOK, D), f32),
            jax.ShapeDtypeStruct((N_TOK, D), h2_dtype),
        ],
        scratch_shapes=[pltpu.VMEM((D, D), bf16)],
        compiler_params=_cparams(("arbitrary",)),
        name="out_proj",
    )(mixcat, w_out, x, modp, mods, modp, mods, modp, mods, g_ffn.reshape(DEPTH, 1, D))


def _group_ffn_kernel(ge_ref, gn_ref, rs_ref, tail_ref, x_hbm, wg_ref, wu_ref, wd_ref, y_hbm,
                      xbf, yacc, xstage, wgs, wus, wds, sem, *, sub, nsub, nj, all_full, chunk):
    del ge_ref
    g = pl.program_id(0)
    j = pl.program_id(1)
    n = gn_ref[g]

    @pl.when(jnp.logical_and(g == 0, j == 0))
    def _():
        yacc[0] = jnp.zeros((sub, D), f32)

        def tail_copy(c):
            r0 = pl.multiple_of(tail_ref[0] + c * sub, 8)
            return pltpu.make_async_copy(yacc.at[0], y_hbm.at[pl.ds(r0, sub), :], sem.at[1, 0])

        def start(c, carry):
            tail_copy(c).start()
            return carry

        def wait(c, carry):
            tail_copy(c).wait()
            return carry

        lax.fori_loop(0, tail_ref[1], start, 0)
        lax.fori_loop(0, tail_ref[1], wait, 0)

    def x_copy(m):
        r0 = pl.multiple_of(rs_ref[g * nsub + m], 8)
        return pltpu.make_async_copy(x_hbm.at[pl.ds(r0, sub), :], xstage.at[m % 2], sem.at[0, m % 2])

    def y_copy(grp, m):
        r0 = pl.multiple_of(rs_ref[grp * nsub + m], 8)
        return pltpu.make_async_copy(yacc.at[m], y_hbm.at[pl.ds(r0, sub), :], sem.at[1, m])

    def for_each_subtile(count, fn):
        def body(m, carry):
            fn(m)
            return carry
        lax.fori_loop(0, count, body, 0)

    @pl.when(jnp.logical_and(n > 0, j == 0))
    def _():
        x_copy(0).start()

        def land(m):
            @pl.when(m + 1 < n)
            def _():
                x_copy(m + 1).start()
            x_copy(m).wait()
            xbf[m] = xstage[m % 2].astype(bf16)
        for_each_subtile(n, land)

        @pl.when(g > 0)
        def _():
            for_each_subtile(gn_ref[jnp.maximum(g - 1, 0)], lambda m: y_copy(g - 1, m).wait())

        def clear(m):
            yacc[m] = jnp.zeros((sub, D), f32)
        for_each_subtile(n, clear)

    def swiglu_rows(m0, k, wg, wu, wd):
        x = xbf[pl.ds(m0, k)].reshape(k * sub, D)
        gate = jnp.dot(x, wg, preferred_element_type=f32)
        up = jnp.dot(x, wu, preferred_element_type=f32)
        act = (gate * _sigmoid(gate) * up).astype(bf16)
        yacc[pl.ds(m0, k)] += jnp.dot(act, wd, preferred_element_type=f32).reshape(k, sub, D)

    def narrow_weights_and_first(k):
        wg = wg_ref[...].astype(bf16)
        wu = wu_ref[...].astype(bf16)
        wd = wd_ref[...].astype(bf16)
        wgs[...] = wg
        wus[...] = wu
        wds[...] = wd
        swiglu_rows(0, k, wg, wu, wd)

    if all_full:
        @pl.when(n > 0)
        def _():
            narrow_weights_and_first(1)
            for m in range(1, nsub):
                swiglu_rows(m, 1, wgs[...], wus[...], wds[...])
    else:
        n_wide = n // chunk
        n_tail = n - chunk * n_wide

        @pl.when(n_wide > 0)
        def _():
            narrow_weights_and_first(chunk)

            def wide(t, carry):
                swiglu_rows(t * chunk, chunk, wgs[...], wus[...], wds[...])
                return carry
            lax.fori_loop(1, n_wide, wide, 0)
            for k in range(1, chunk):
                @pl.when(n_tail == k)
                def _():
                    swiglu_rows(n_wide * chunk, k, wgs[...], wus[...], wds[...])

        for k in range(1, chunk):
            @pl.when(jnp.logical_and(n_wide == 0, n_tail == k))
            def _():
                narrow_weights_and_first(k)

    @pl.when(jnp.logical_and(n > 0, j == nj - 1))
    def _():
        for_each_subtile(n, lambda m: y_copy(g, m).start())
        last = g + 1 >= pl.num_programs(0)
        next_n = gn_ref[jnp.minimum(g + 1, pl.num_programs(0) - 1)]

        @pl.when(jnp.logical_or(last, next_n == 0))
        def _():
            for_each_subtile(n, lambda m: y_copy(g, m).wait())


def _group_ffn_call(name, layer, group_expert, group_n, row_start, tail, x, wg, wu, wd, out_rows, sub, nsub,
                    n_groups, all_full, chunk=1):
    d_ff = wg.shape[-1]
    nj = d_ff // TF_FFN

    def jmap(g, j, gn):
        return jnp.where(gn[g] > 0, j, nj - 1)

    return pl.pallas_call(
        functools.partial(_group_ffn_kernel, sub=sub, nsub=nsub, nj=nj, all_full=all_full, chunk=chunk),
        grid_spec=pltpu.PrefetchScalarGridSpec(
            num_scalar_prefetch=4,
            grid=(n_groups, nj),
            in_specs=[
                pl.BlockSpec(memory_space=pl.ANY),
                pl.BlockSpec((None, None, D, TF_FFN), lambda g, j, ge, gn, rs, tl: (layer, ge[g], 0, jmap(g, j, gn))),
                pl.BlockSpec((None, None, D, TF_FFN), lambda g, j, ge, gn, rs, tl: (layer, ge[g], 0, jmap(g, j, gn))),
                pl.BlockSpec((None, None, TF_FFN, D), lambda g, j, ge, gn, rs, tl: (layer, ge[g], jmap(g, j, gn), 0)),
            ],
            out_specs=pl.BlockSpec(memory_space=pl.ANY),
            scratch_shapes=[
                pltpu.VMEM((nsub, sub, D), bf16),
                pltpu.VMEM((nsub, sub, D), f32),
                pltpu.VMEM((2, sub, D), f32),
                pltpu.VMEM((D, TF_FFN), bf16),
                pltpu.VMEM((D, TF_FFN), bf16),
                pltpu.VMEM((TF_FFN, D), bf16),
                pltpu.SemaphoreType.DMA((2, nsub)),
            ],
        ),
        out_shape=jax.ShapeDtypeStruct((out_rows, D), f32),
        compiler_params=_cparams(("arbitrary", "arbitrary")),
        name=name,
    )(group_expert, group_n, row_start, tail, x, wg, wu, wd)


def _dense_ffn_call(l, h2, wg, wu, wd):
    n_groups = N_TOK // (SUB_DENSE * NSUB_DENSE)
    zeros = jnp.zeros((n_groups,), i32)
    return _group_ffn_call(
        "ffn_dense", l // 2, zeros, zeros + NSUB_DENSE, jnp.arange(n_groups * NSUB_DENSE, dtype=i32) * SUB_DENSE,
        jnp.zeros((2,), i32), h2, wg[:, None], wu[:, None], wd[:, None], N_TOK, SUB_DENSE, NSUB_DENSE, n_groups,
        True)


def _residual_out(i, tm, x2, ep, outs, final):
    if final:
        (g_ref,), (yp_ref, ys_ref) = ep, outs
        y = _rms(x2, g_ref[...])
        npt = N_PROMPT // tm

        @pl.when(i < npt)
        def _():
            yp_ref[...] = y

        @pl.when(i >= npt)
        def _():
            ys_ref[...] = y
    else:
        (shp, shs, scp, scs, g_ref), (x_ref, h_ref) = ep, outs
        x_ref[...] = x2
        h_ref[...] = _modulated_norm(i, tm, x2, g_ref, shp, shs, scp, scs)


def _residual_specs(l, tm, nargs, g_mix, g_final):
    if nargs == 1:
        tile = lambda i: (i, 0)
        const3 = lambda i: (l + 1, 0, 0)
        const2 = lambda i: (0, 0)
        pmap = lambda i: (jnp.minimum(i, N_PROMPT // tm - 1), 0)
        smap = lambda i: (jnp.maximum(i - N_PROMPT // tm, 0), 0)
    else:
        tile = lambda i, s: (i, 0)
        const3 = lambda i, s: (l + 1, 0, 0)
        const2 = lambda i, s: (0, 0)
        pmap = lambda i, s: (jnp.minimum(i, N_PROMPT // tm - 1), 0)
        smap = lambda i, s: (jnp.maximum(i - N_PROMPT // tm, 0), 0)
    if l == DEPTH - 1:
        return ([g_final.reshape(1, D)], [pl.BlockSpec((1, D), const2)],
                [pl.BlockSpec((tm, D), pmap), pl.BlockSpec((tm, D), smap)],
                [jax.ShapeDtypeStruct((N_PROMPT, D), f32), jax.ShapeDtypeStruct((N_SAMPLE, D), f32)])
    mod_nargs = 1 if nargs == 1 else 3
    return (None, [*_mod_specs(l + 1, 0, tm, mod_nargs), *_mod_specs(l + 1, 1, tm, mod_nargs),
                   pl.BlockSpec((None, 1, D), const3)],
            [pl.BlockSpec((tm, D), tile), pl.BlockSpec((tm, D), tile)],
            [jax.ShapeDtypeStruct((N_TOK, D), f32), jax.ShapeDtypeStruct((N_TOK, D), bf16)])


def _dense_residual_kernel(y_ref, x1_ref, gtp, gts, *rest, final):
    i = pl.program_id(0)
    n_ep = 1 if final else 5
    x2 = x1_ref[...] + _sel_mod(i, TM, gtp, gts) * y_ref[...]
    _residual_out(i, TM, x2, rest[:n_ep], rest[n_ep:], final)


def _dense_residual_call(l, y, x1, modp, mods, g_mix, g_final):
    final = l == DEPTH - 1
    gtp, gts = _mod_specs(l, 5, TM, 1)
    ep_ops, ep_specs, out_specs, out_shape = _residual_specs(l, TM, 1, g_mix, g_final)
    if ep_ops is None:
        ep_ops = [modp, mods, modp, mods, g_mix.reshape(DEPTH, 1, D)]
    return pl.pallas_call(
        functools.partial(_dense_residual_kernel, final=final),
        grid=(N_TOK // TM,),
        in_specs=[pl.BlockSpec((TM, D), lambda i: (i, 0)), pl.BlockSpec((TM, D), lambda i: (i, 0)), gtp, gts, *ep_specs],
        out_specs=out_specs,
        out_shape=out_shape,
        compiler_params=_cparams(("arbitrary",)),
        name="ffn_residual",
    )(y, x1, modp, mods, *ep_ops)


def _router_kernel(h_ref, wr_ref, route_ref, cnt_ref, carry):
    i = pl.program_id(0)

    @pl.when(i == 0)
    def _():
        carry[...] = jnp.zeros((8, 128), f32)

    h = h_ref[...]
    hh = h.astype(bf16)
    hl = (h - hh.astype(f32)).astype(bf16)
    w = wr_ref[...]
    wh = w.astype(bf16)
    wl = (w - wh.astype(f32)).astype(bf16)
    logits = (jnp.dot(hh, wh, preferred_element_type=f32) + jnp.dot(hl, wh, preferred_element_type=f32)
              + jnp.dot(hh, wl, preferred_element_type=f32))
    lane = lax.broadcasted_iota(i32, (TM, 128), 1)
    neg = jnp.float32(-jnp.inf)
    lg = jnp.where(lane < N_EXPERTS, logits, neg)
    m1 = jnp.max(lg, axis=-1, keepdims=True)
    i1 = jnp.min(jnp.where(lg == m1, lane, 128), axis=-1, keepdims=True)
    lg2 = jnp.where(lane == i1, neg, lg)
    m2 = jnp.max(lg2, axis=-1, keepdims=True)
    i2 = jnp.min(jnp.where(lg2 == m2, lane, 128), axis=-1, keepdims=True)
    e2 = jnp.exp(m2 - m1)
    g1 = 1.0 / (1.0 + e2)
    g2 = e2 / (1.0 + e2)

    hit1 = lane == i1
    hit2 = lane == i2
    cnt = (hit1 | hit2).astype(f32)
    before = (lax.broadcasted_iota(i32, (TM, TM), 0) > lax.broadcasted_iota(i32, (TM, TM), 1))
    prefix = jnp.dot(before.astype(bf16), cnt.astype(bf16), preferred_element_type=f32) + carry[0:1, :]
    r1 = jnp.sum(jnp.where(hit1, prefix, 0.0), axis=-1, keepdims=True)
    r2 = jnp.sum(jnp.where(hit2, prefix, 0.0), axis=-1, keepdims=True)
    total = carry[...] + jnp.sum(cnt, axis=0, keepdims=True)
    carry[...] = total
    cnt_ref[...] = total

    out = jnp.where(lane == 0, i1.astype(f32), 0.0)
    out = jnp.where(lane == 1, i2.astype(f32), out)
    out = jnp.where(lane == 2, g1, out)
    out = jnp.where(lane == 3, g2, out)
    out = jnp.where(lane == 4, r1, out)
    out = jnp.where(lane == 5, r2, out)
    route_ref[...] = out


def _router_call(h2, wr_pad):
    return pl.pallas_call(
        _router_kernel,
        grid=(N_TOK // TM,),
        in_specs=[
            pl.BlockSpec((TM, D), lambda i: (i, 0)),
            pl.BlockSpec((D, 128), lambda i: (0, 0)),
        ],
        out_specs=[
            pl.BlockSpec((TM, 128), lambda i: (i, 0)),
            pl.BlockSpec((8, 128), lambda i: (0, 0)),
        ],
        out_shape=[
            jax.ShapeDtypeStruct((N_TOK, 128), f32),
            jax.ShapeDtypeStruct((8, 128), f32),
        ],
        scratch_shapes=[pltpu.VMEM((8, 128), f32)],
        compiler_params=_cparams(("arbitrary",)),
        name="moe_router",
    )(h2, wr_pad)


def _dispatch_kernel(pos_ref, pad_ref, h_ref, xs_ref, zbuf, sem, zsem, tsem):
    i = pl.program_id(0)

    def zero_copy(r):
        return pltpu.make_async_copy(zbuf.at[pl.ds(0, 1), :], xs_ref.at[pl.ds(r, 1), :], zsem)

    def tail_copy(c):
        r0 = pl.multiple_of(pad_ref[2 * N_EXPERTS] + c * SUB_MOE, 8)
        return pltpu.make_async_copy(zbuf, xs_ref.at[pl.ds(r0, SUB_MOE), :], tsem)

    def for_each_pad_row(fn):
        for e in range(N_EXPERTS):
            def body(r, carry):
                fn(r)
                return carry
            lax.fori_loop(pad_ref[e], pad_ref[N_EXPERTS + e], body, 0)

    def for_each_tail_tile(fn):
        def body(c, carry):
            fn(c)
            return carry
        lax.fori_loop(0, pad_ref[2 * N_EXPERTS + 1], body, 0)

    @pl.when(i == 0)
    def _():
        zbuf[...] = jnp.zeros((SUB_MOE, D), f32)
        for_each_pad_row(lambda r: zero_copy(r).start())
        for_each_tail_tile(lambda c: tail_copy(c).start())

    def row_copy(r, slot):
        p = pos_ref[2 * (i * TM + r) + slot]
        return pltpu.make_async_copy(h_ref.at[pl.ds(r, 1), :], xs_ref.at[pl.ds(p, 1), :], sem)

    def issue(r, carry):
        row_copy(r, 0).start(priority=0)
        row_copy(r, 1).start(priority=1)
        return carry

    lax.fori_loop(0, TM, issue, 0, unroll=8)
    for _ in range(2):
        pltpu.make_async_copy(h_ref, xs_ref.at[pl.ds(0, TM), :], sem).wait()

    @pl.when(i == 0)
    def _():
        for_each_pad_row(lambda r: zero_copy(r).wait())
        for_each_tail_tile(lambda c: tail_copy(c).wait())


def _dispatch_call(pos_flat, pad_bounds, h2):
    return pl.pallas_call(
        _dispatch_kernel,
        grid_spec=pltpu.PrefetchScalarGridSpec(
            num_scalar_prefetch=2,
            grid=(N_TOK // TM,),
            in_specs=[pl.BlockSpec((TM, D), lambda i, pos, pad: (i, 0))],
            out_specs=pl.BlockSpec(memory_space=pl.ANY),
            scratch_shapes=[pltpu.VMEM((SUB_MOE, D), f32), pltpu.SemaphoreType.DMA(()),
                            pltpu.SemaphoreType.DMA(()), pltpu.SemaphoreType.DMA(())],
        ),
        out_shape=jax.ShapeDtypeStruct((P_ROWS, D), f32),
        compiler_params=_cparams(("arbitrary",)),
        name="moe_dispatch",
    )(pos_flat, pad_bounds, h2)


def _combine_kernel(pos_ref, y_ref, x1_ref, route_ref, gtp, gts, *rest, final):
    n_ep = 1 if final else 5
    ep, outs, (buf, sem) = rest[:n_ep], rest[n_ep:-2], rest[-2:]
    i = pl.program_id(0)
    nt = pl.num_programs(0)
    slot = i % 2

    def start_tile(t, s):
        def issue(r, carry):
            for k in range(2):
                p = pos_ref[2 * (t * TM_COMB + r) + k]
                pltpu.make_async_copy(y_ref.at[pl.ds(p, 1), :], buf.at[s, k, pl.ds(r, 1), :],
                                      sem.at[s]).start(priority=k)
            return carry
        lax.fori_loop(0, TM_COMB, issue, 0, unroll=8)

    @pl.when(i == 0)
    def _():
        start_tile(0, 0)

    @pl.when(i + 1 < nt)
    def _():
        start_tile(i + 1, 1 - slot)

    for k in range(2):
        pltpu.make_async_copy(y_ref.at[pl.ds(0, TM_COMB), :], buf.at[slot, k], sem.at[slot]).wait()
    f = route_ref[:, 2:3] * buf[slot, 0] + route_ref[:, 3:4] * buf[slot, 1]
    x2 = x1_ref[...] + _sel_mod(i, TM_COMB, gtp, gts) * f
    _residual_out(i, TM_COMB, x2, ep, outs, final)


def _combine_call(l, pos_flat, y, x1, route, modp, mods, g_mix, g_final):
    final = l == DEPTH - 1
    gtp, gts = _mod_specs(l, 5, TM_COMB, 3)
    ep_ops, ep_specs, out_specs, out_shape = _residual_specs(l, TM_COMB, 2, g_mix, g_final)
    if ep_ops is None:
        ep_ops = [modp, mods, modp, mods, g_mix.reshape(DEPTH, 1, D)]
    return pl.pallas_call(
        functools.partial(_combine_kernel, final=final),
        grid_spec=pltpu.PrefetchScalarGridSpec(
            num_scalar_prefetch=1,
            grid=(N_TOK // TM_COMB,),
            in_specs=[
                pl.BlockSpec(memory_space=pl.ANY),
                pl.BlockSpec((TM_COMB, D), lambda i, pos: (i, 0)),
                pl.BlockSpec((TM_COMB, 128), lambda i, pos: (i, 0)),
                gtp, gts, *ep_specs,
            ],
            out_specs=out_specs,
            scratch_shapes=[
                pltpu.VMEM((2, 2, TM_COMB, D), f32),
                pltpu.SemaphoreType.DMA((2,)),
            ],
        ),
        out_shape=out_shape,
        compiler_params=_cparams(("arbitrary",)),
        name="moe_combine",
    )(pos_flat, y, x1, route, modp, mods, *ep_ops)


def _moe_layer(l, h2, x1, w_router, wg, wu, wd, modp, mods, g_mix, g_final):
    lm = l // 2
    wr_pad = jnp.pad(w_router[lm], ((0, 0), (0, 128 - N_EXPERTS)))
    route, cnt = _router_call(h2, wr_pad)
    expert = route[:, 0:2].astype(i32)
    rank = route[:, 4:6].astype(i32)
    counts = cnt[0, :N_EXPERTS].astype(i32)
    tiles = (counts + SUB_MOE - 1) // SUB_MOE
    tile_start = jnp.cumsum(tiles) - tiles
    row_start = tile_start * SUB_MOE
    pos_flat = (row_start[expert] + rank).reshape(-1)
    n_tiles = jnp.sum(tiles)
    tail = jnp.stack([n_tiles * SUB_MOE, MAX_TILES - n_tiles])
    pad_bounds = jnp.concatenate([row_start + counts, row_start + tiles * SUB_MOE, tail])
    groups = (tiles + NSUB_MOE - 1) // NSUB_MOE
    group_end = jnp.cumsum(groups)
    n_groups = group_end[-1]
    gid = jnp.arange(MAX_GROUPS, dtype=i32)
    gcl = jnp.minimum(gid, n_groups - 1)
    group_expert = jnp.sum((gcl[:, None] >= group_end[None, :]).astype(i32), axis=1)
    per_group = tiles // jnp.maximum(groups, 1)
    extra = tiles - per_group * groups
    k = gcl - (group_end - groups)[group_expert]
    first = k * per_group[group_expert] + jnp.minimum(k, extra[group_expert])
    group_n = jnp.where(gid < n_groups, per_group[group_expert] + (k < extra[group_expert]).astype(i32), 0)
    sub_start = (tile_start[group_expert] + first)[:, None] + jnp.arange(NSUB_MOE, dtype=i32)[None, :]
    xs = _dispatch_call(pos_flat, pad_bounds, h2)
    y = _group_ffn_call("moe_experts", lm, group_expert, group_n, (sub_start * SUB_MOE).reshape(-1), tail, xs,
                        wg, wu, wd, P_ROWS, SUB_MOE, NSUB_MOE, MAX_GROUPS, False, CHUNK_MOE)
    return _combine_call(l, pos_flat, y, x1, route, modp, mods, g_mix, g_final)


def kernel(x_prompt, x_sample, state_pool, state_conv, c_prompt, c_sample, w_ada, b_ada, g_mix, g_ffn, w_in, w_pool, pool_scale, w_dw, b_dw, conv_ln_g, conv_ln_b, sgu_ln_g, sgu_ln_b, w_spatial, b_spatial, w_out, w_ff_gate, w_ff_up, w_ff_down, w_router, w_exp_gate, w_exp_up, w_exp_down, g_final):
    c_all = jnp.concatenate([jnp.repeat(c_sample, DEC_SEQ, axis=0), c_prompt,
                             jnp.zeros((N_SEQ_PAD - N_SAMPLE - BATCH, D), f32)], axis=0)
    modp = mods = _ada_call(c_all, w_ada, b_ada)
    hp_t = state_pool.transpose(0, 2, 1, 3)
    hc_t = state_conv.transpose(0, 2, 1, 3)
    bs_bcast = jnp.broadcast_to(b_spatial[:, :, :, None], (DEPTH, N_HEADS, CHUNK, CHUNK))
    ws_small = w_spatial[:, :, :DEC_SEQ, :DEC_SEQ].reshape(DEPTH, -1)
    bs_small = b_spatial[:, :, :DEC_SEQ].reshape(DEPTH, -1)
    w_in_bf = w_in.astype(bf16)

    x, h1 = _prenorm_call(x_prompt, x_sample, modp, mods, g_mix)
    pool_p, conv_p, pool_s, conv_s, chunk_v = [], [], [], [], []
    for l in range(DEPTH):
        proj = _inproj_call(l, h1, w_in_bf)
        proj_s = proj[N_PROMPT:].reshape(DEC_BATCH, DEC_SEQ, P_IN)
        mix_s, glu_s, v_s = _mixs_call(l, ws_small[l], bs_small[l], proj_s.transpose(1, 0, 2), hp_t, hc_t,
                                       w_pool, pool_scale, w_dw, b_dw, conv_ln_g, conv_ln_b, sgu_ln_g, sgu_ln_b)
        mixcat, npool, nconv = _mixp_call(l, proj, mix_s.transpose(1, 0, 2).reshape(N_SAMPLE, D), w_pool, pool_scale,
                                          w_dw, b_dw, conv_ln_g, conv_ln_b, sgu_ln_g, sgu_ln_b, w_spatial, bs_bcast)
        pool_p.append(npool[:, POOL_HALO - POOL_HIST:])
        conv_p.append(nconv[:, CONV_HALO - CONV_HIST:])
        pool_s.append(jnp.concatenate([state_pool[l][:, DEC_SEQ:], proj_s[:, :, :D_POOL]], axis=1))
        conv_s.append(jnp.concatenate([state_conv[l][:, DEC_SEQ:], glu_s.transpose(1, 0, 2)], axis=1))
        chunk_v.append(v_s.transpose(1, 0, 2))
        x1, h2 = _outproj_call(l, mixcat, w_out, x, modp, mods, g_ffn, f32)
        if l % 2 == 0:
            y = _dense_ffn_call(l, h2, w_ff_gate, w_ff_up, w_ff_down)
            out = _dense_residual_call(l, y, x1, modp, mods, g_mix, g_final)
        else:
            out = _moe_layer(l, h2, x1, w_router, w_exp_gate, w_exp_up, w_exp_down, modp, mods, g_mix, g_final)
        if l < DEPTH - 1:
            x, h1 = out
    y_prompt, y_sample = out
    return (y_prompt.reshape(BATCH, SEQ, D), y_sample.reshape(DEC_BATCH, DEC_SEQ, D),
            jnp.stack(pool_p), jnp.stack(conv_p), jnp.stack(pool_s), jnp.stack(conv_s), jnp.stack(chunk_v))
```

```python
import functools

import jax
import jax.numpy as jnp
from jax import lax
from jax.experimental import pallas as pl
from jax.experimental.pallas import tpu as pltpu

f32 = jnp.float32
bf16 = jnp.bfloat16
i32 = jnp.int32

D = 2048
BATCH = 4
SEQ = 2048
DEPTH = 4
DEC_BATCH = 128
DEC_SEQ = 4
PAST_LEN = 16384
D_POOL = 512
POOL_WINDOWS = (2, 4, 8, 16)
POOL_GROUP = 128
POOL_HIST = 15
D_CONV = 768
CONV_WIDTH = 31
CONV_HIST = 30
D_CHUNK = 768
CHUNK = 128
N_HEADS = 6
P_IN = 3584
D_FF = 5632
N_EXPERTS = 8
D_FF_EXPERT = 7168
EPS = 1e-6

N_PROMPT = BATCH * SEQ
N_SAMPLE = DEC_BATCH * DEC_SEQ
N_TOK = N_PROMPT + N_SAMPLE
N_SEQ = BATCH + DEC_BATCH
N_SEQ_PAD = N_SAMPLE + 8

C1 = D_POOL
C2 = C1 + D_CONV
C3 = C2 + D_CONV
C4 = C3 + D_CHUNK

VMEM_LIMIT = 56 * 1024 * 1024
TM = 512
TM_OUT = 256
TN_ADA = 1024
TT = 512
ROW_CHUNK = 64
POOL_HALO = 16
CONV_HALO = 32
SB = 32
TF_FFN = 256
SUB_MOE = 128
NSUB_MOE = 22
CHUNK_MOE = 6
SUB_DENSE = N_TOK // 16
NSUB_DENSE = 4
TM_COMB = 256
MAX_TILES = (N_TOK * 2) // SUB_MOE + N_EXPERTS
MAX_GROUPS = (MAX_TILES + (NSUB_MOE - 1) * N_EXPERTS) // NSUB_MOE
P_ROWS = MAX_TILES * SUB_MOE


def _cparams(sem):
    return pltpu.CompilerParams(dimension_semantics=sem, vmem_limit_bytes=VMEM_LIMIT)


def _sigmoid(x):
    return jax.nn.sigmoid(x)


def _rms(x, g):
    return x * lax.rsqrt(jnp.mean(x * x, axis=-1, keepdims=True) + EPS) * g


def _layer_norm(y, g, b):
    mu = jnp.mean(y, axis=-1, keepdims=True)
    d = y - mu
    var = jnp.mean(d * d, axis=-1, keepdims=True)
    return d * lax.rsqrt(var + EPS) * g + b


def _sel_mod(i, tm, p_ref, s_ref):
    seq = jnp.minimum(i // (SEQ // tm), BATCH - 1)
    return jnp.where(i >= N_PROMPT // tm, s_ref[...], p_ref[pl.ds(seq, 1), :])


def _mod_specs(l, chunk, tm, nargs):
    npt = N_PROMPT // tm
    pblk = N_SAMPLE // 8
    if nargs == 1:
        mp = lambda i: (l, pblk, chunk)
        ms = lambda i: (l, jnp.maximum(i - npt, 0), chunk)
    elif nargs == 2:
        mp = lambda i, j: (l, pblk, chunk)
        ms = lambda i, j: (l, jnp.maximum(i - npt, 0), chunk)
    else:
        mp = lambda i, s: (l, pblk, chunk)
        ms = lambda i, s: (l, jnp.maximum(i - npt, 0), chunk)
    return pl.BlockSpec((None, 8, D), mp), pl.BlockSpec((None, tm, D), ms, pipeline_mode=pl.Buffered(1))


def _ada_kernel(c_ref, w_ref, b_ref, o_ref):
    c = c_ref[...]
    s = (c * _sigmoid(c)).astype(bf16)
    mod = jnp.dot(s, w_ref[...].astype(bf16), preferred_element_type=f32) + b_ref[...]
    seq_mod = mod[0:DEC_BATCH]
    hi = seq_mod.astype(bf16)
    rest = seq_mod - hi.astype(f32)
    mid = rest.astype(bf16)
    lo = (rest - mid.astype(f32)).astype(bf16)
    token = lax.broadcasted_iota(i32, (N_SAMPLE, DEC_BATCH), 0)
    owner = lax.broadcasted_iota(i32, (N_SAMPLE, DEC_BATCH), 1)
    rep = (token // DEC_SEQ == owner).astype(bf16)
    expand = lambda part: jnp.dot(rep, part, preferred_element_type=f32)
    o_ref[0:N_SAMPLE, :] = (expand(hi) + expand(mid)) + expand(lo)
    o_ref[N_SAMPLE:N_SEQ_PAD, :] = mod[DEC_BATCH:DEC_BATCH + N_SEQ_PAD - N_SAMPLE]


def _ada_call(c_all, w_ada, b_ada):
    nj = 6 * D // TN_ADA
    return pl.pallas_call(
        _ada_kernel,
        grid=(DEPTH, nj),
        in_specs=[
            pl.BlockSpec((DEC_BATCH + N_SEQ_PAD - N_SAMPLE, D), lambda l, j: (0, 0)),
            pl.BlockSpec((None, D, TN_ADA), lambda l, j: (l, 0, j)),
            pl.BlockSpec((None, 1, TN_ADA), lambda l, j: (l, 0, j)),
        ],
        out_specs=pl.BlockSpec((None, N_SEQ_PAD, TN_ADA), lambda l, j: (l, 0, j)),
        out_shape=jax.ShapeDtypeStruct((DEPTH, N_SEQ_PAD, 6 * D), f32),
        compiler_params=_cparams(("arbitrary", "arbitrary")),
        name="ada_mod",
    )(c_all, w_ada, b_ada.reshape(DEPTH, 1, 6 * D))


def _modulated_norm(i, tm, x, g_ref, shp, shs, scp, scs):
    h = _rms(x, g_ref[...]) * (1.0 + _sel_mod(i, tm, scp, scs)) + _sel_mod(i, tm, shp, shs)
    return h.astype(bf16)


def _prenorm_kernel(xp_ref, xs_ref, shp, shs, scp, scs, g_ref, x_ref, h_ref):
    i = pl.program_id(0)
    x = jnp.where(i >= N_PROMPT // TM, xs_ref[...], xp_ref[...])
    x_ref[...] = x
    h_ref[...] = _modulated_norm(i, TM, x, g_ref, shp, shs, scp, scs)


def _prenorm_call(x_prompt, x_sample, modp, mods, g_mix):
    assert N_SAMPLE == TM
    shp, shs = _mod_specs(0, 0, TM, 1)
    scp, scs = _mod_specs(0, 1, TM, 1)
    npt = N_PROMPT // TM
    return pl.pallas_call(
        _prenorm_kernel,
        grid=(N_TOK // TM,),
        in_specs=[
            pl.BlockSpec((TM, D), lambda i: (jnp.minimum(i, npt - 1), 0)),
            pl.BlockSpec((TM, D), lambda i: (0, 0), pipeline_mode=pl.Buffered(1)),
            shp, shs, scp, scs,
            pl.BlockSpec((None, 1, D), lambda i: (0, 0, 0)),
        ],
        out_specs=[pl.BlockSpec((TM, D), lambda i: (i, 0)), pl.BlockSpec((TM, D), lambda i: (i, 0))],
        out_shape=[jax.ShapeDtypeStruct((N_TOK, D), f32), jax.ShapeDtypeStruct((N_TOK, D), bf16)],
        compiler_params=_cparams(("arbitrary",)),
        name="prenorm",
    )(x_prompt.reshape(N_PROMPT, D), x_sample.reshape(N_SAMPLE, D), modp, mods, modp, mods,
      g_mix.reshape(DEPTH, 1, D))


def _inproj_kernel(h_ref, w_ref, o_ref):
    o_ref[...] = jnp.dot(h_ref[...], w_ref[...], preferred_element_type=f32)


def _inproj_call(l, h1, w_in_bf):
    return pl.pallas_call(
        _inproj_kernel,
        grid=(N_TOK // TM,),
        in_specs=[
            pl.BlockSpec((TM, D), lambda i: (i, 0)),
            pl.BlockSpec((None, D, P_IN), lambda i: (l, 0, 0), pipeline_mode=pl.Buffered(1)),
        ],
        out_specs=pl.BlockSpec((TM, P_IN), lambda i: (i, 0)),
        out_shape=jax.ShapeDtypeStruct((N_TOK, P_IN), f32),
        compiler_params=_cparams(("arbitrary",)),
        name="in_proj",
    )(h1, w_in_bf)


def _mixp_kernel(p_ref, ms_ref, wpool_ref, pscale_ref, wdw_ref, bdw_ref, clg_ref, clb_ref, slg_ref, slb_ref,
                 ws_ref, bs_ref, o_ref, *state_and_scratch):
    b = pl.program_id(0)

    @pl.when(b < BATCH)
    def _():
        _mixp_body(p_ref, wpool_ref, pscale_ref, wdw_ref, bdw_ref, clg_ref, clb_ref, slg_ref, slb_ref,
                   ws_ref, bs_ref, o_ref, *state_and_scratch)

    @pl.when(jnp.logical_and(b == BATCH, pl.program_id(1) < N_SAMPLE // TT))
    def _():
        o_ref[...] = ms_ref[...]


def _mixp_body(p_ref, wpool_ref, pscale_ref, wdw_ref, bdw_ref, clg_ref, clb_ref, slg_ref, slb_ref,
               ws_ref, bs_ref, o_ref, npool_ref, nconv_ref, pext, cext, ybuf):
    t = pl.program_id(1)

    @pl.when(t == 0)
    def _():
        pext[0:POOL_HALO, :] = jnp.zeros((POOL_HALO, D_POOL), f32)
        cext[0:CONV_HALO, :] = jnp.zeros((CONV_HALO, D_CONV), f32)

    pext[POOL_HALO:POOL_HALO + TT, :] = p_ref[:, 0:C1]
    pos = lax.broadcasted_iota(i32, (TT, POOL_GROUP), 0) + t * TT
    for g, w in enumerate(POOL_WINDOWS):
        sl = slice(g * POOL_GROUP, (g + 1) * POOL_GROUP)
        tok = pext[POOL_HALO:POOL_HALO + TT, sl]
        s = tok
        for j in range(1, w):
            s = s + pext[POOL_HALO - j:POOL_HALO - j + TT, sl]
        cnt = jnp.minimum(pos + 1, w).astype(f32)
        diff = (s / cnt - tok).astype(bf16)
        o = jnp.dot(diff, wpool_ref[g].astype(bf16), preferred_element_type=f32) * pscale_ref[:, sl]
        o_ref[:, sl] = o.astype(bf16)
    tail = pext[TT:TT + POOL_HALO, :]
    npool_ref[...] = tail
    pext[0:POOL_HALO, :] = tail

    cext[CONV_HALO:CONV_HALO + TT, :] = p_ref[:, C1:C2] * _sigmoid(p_ref[:, C2:C3])
    off = CONV_HALO - CONV_HIST
    for c in range(D_CONV // 128):
        cs = slice(c * 128, (c + 1) * 128)
        for r in range(TT // ROW_CHUNK):
            r0 = r * ROW_CHUNK
            acc = bdw_ref[:, cs]
            for res in range(8):
                rows = ROW_CHUNK if res == 0 else ROW_CHUNK + 8
                z = None
                for a in range((CONV_WIDTH + off) // 8 + 1):
                    k = 8 * a + res - off
                    if 0 <= k < CONV_WIDTH:
                        term = cext[r0 + 8 * a:r0 + 8 * a + rows, cs] * wdw_ref[k:k + 1, cs]
                        z = term if z is None else z + term
                acc = acc + (z if res == 0 else pltpu.roll(z, rows - res, axis=0)[0:ROW_CHUNK])
            ybuf[r0:r0 + ROW_CHUNK, cs] = acc
    yn = _layer_norm(ybuf[...], clg_ref[...], clb_ref[...])
    o_ref[:, C1:C2] = (yn * _sigmoid(yn)).astype(bf16)
    ctail = cext[TT:TT + CONV_HALO, :]
    nconv_ref[...] = ctail
    cext[0:CONV_HALO, :] = ctail

    ybuf[...] = _layer_norm(jax.nn.gelu(p_ref[:, C4:P_IN]), slg_ref[...], slb_ref[...])
    causal = (lax.broadcasted_iota(i32, (CHUNK, CHUNK), 0) >= lax.broadcasted_iota(i32, (CHUNK, CHUNK), 1))
    for h in range(N_HEADS):
        hs = slice(h * 128, (h + 1) * 128)
        wm = jnp.where(causal, ws_ref[h], 0.0).astype(bf16)
        for n in range(TT // CHUNK):
            rs = slice(n * CHUNK, (n + 1) * CHUNK)
            s = jnp.dot(wm, ybuf[rs, hs].astype(bf16), preferred_element_type=f32) + bs_ref[h]
            u = jax.nn.gelu(p_ref[rs, C3 + h * 128:C3 + (h + 1) * 128])
            o_ref[rs, C2 + h * 128:C2 + (h + 1) * 128] = (u * s).astype(bf16)


def _mixp_call(l, proj, mix_sample, w_pool, pool_scale, w_dw, b_dw, clg, clb, slg, slb, w_spatial, bs_bcast):
    nt = SEQ // TT
    last = N_TOK // TT - 1
    c2 = lambda b, t: (l, 0, 0)
    tile = lambda b, t: (jnp.minimum(b * nt + t, last), 0)
    seq = lambda b, t: (jnp.minimum(b, BATCH - 1), 0, 0)
    return pl.pallas_call(
        _mixp_kernel,
        grid=(BATCH + 1, nt),
        in_specs=[
            pl.BlockSpec((TT, P_IN), tile),
            pl.BlockSpec((TT, D), lambda b, t: (jnp.where(b == BATCH, jnp.minimum(t, N_SAMPLE // TT - 1), 0), 0)),
            pl.BlockSpec((None, 4, POOL_GROUP, POOL_GROUP), lambda b, t: (l, 0, 0, 0)),
            pl.BlockSpec((None, 1, D_POOL), c2),
            pl.BlockSpec((None, CONV_WIDTH, D_CONV), c2),
            pl.BlockSpec((None, 1, D_CONV), c2),
            pl.BlockSpec((None, 1, D_CONV), c2),
            pl.BlockSpec((None, 1, D_CONV), c2),
            pl.BlockSpec((None, 1, D_CHUNK), c2),
            pl.BlockSpec((None, 1, D_CHUNK), c2),
            pl.BlockSpec((None, N_HEADS, CHUNK, CHUNK), lambda b, t: (l, 0, 0, 0)),
            pl.BlockSpec((None, N_HEADS, CHUNK, CHUNK), lambda b, t: (l, 0, 0, 0)),
        ],
        out_specs=[
            pl.BlockSpec((TT, D), tile),
            pl.BlockSpec((None, POOL_HALO, D_POOL), seq),
            pl.BlockSpec((None, CONV_HALO, D_CONV), seq),
        ],
        out_shape=[
            jax.ShapeDtypeStruct((N_TOK, D), bf16),
            jax.ShapeDtypeStruct((BATCH, POOL_HALO, D_POOL), f32),
            jax.ShapeDtypeStruct((BATCH, CONV_HALO, D_CONV), f32),
        ],
        scratch_shapes=[
            pltpu.VMEM((TT + POOL_HALO, D_POOL), f32),
            pltpu.VMEM((TT + CONV_HALO, D_CONV), f32),
            pltpu.VMEM((TT, D_CONV), f32),
        ],
        compiler_params=_cparams(("arbitrary", "arbitrary")),
        name="mix_prompt",
    )(proj, mix_sample, w_pool, pool_scale.reshape(DEPTH, 1, D_POOL), w_dw, b_dw.reshape(DEPTH, 1, D_CONV),
      clg.reshape(DEPTH, 1, D_CONV), clb.reshape(DEPTH, 1, D_CONV), slg.reshape(DEPTH, 1, D_CHUNK),
      slb.reshape(DEPTH, 1, D_CHUNK), w_spatial, bs_bcast)


def _mixs_kernel(ws_ref, bs_ref, p_ref, hp_ref, hc_ref, wpool_ref, pscale_ref, wdw_ref, bdw_ref,
                 clg_ref, clb_ref, slg_ref, slb_ref, o_ref, glu_ref, v_ref):
    def pool_row(idx, sl):
        if idx < POOL_HIST:
            return hp_ref[idx, :, sl]
        return p_ref[idx - POOL_HIST, :, sl]

    for g, w in enumerate(POOL_WINDOWS):
        sl = slice(g * POOL_GROUP, (g + 1) * POOL_GROUP)
        diffs = []
        for t in range(DEC_SEQ):
            tok = pool_row(POOL_HIST + t, sl)
            s = tok
            for j in range(1, w):
                s = s + pool_row(POOL_HIST + t - j, sl)
            diffs.append(s * (1.0 / w) - tok)
        d = jnp.concatenate(diffs, axis=0).astype(bf16)
        o = jnp.dot(d, wpool_ref[g].astype(bf16), preferred_element_type=f32) * pscale_ref[:, sl]
        for t in range(DEC_SEQ):
            o_ref[t, :, sl] = o[t * SB:(t + 1) * SB].astype(bf16)

    for t in range(DEC_SEQ):
        glu_ref[t] = p_ref[t, :, C1:C2] * _sigmoid(p_ref[t, :, C2:C3])

    def conv_row(idx, cs):
        if idx < CONV_HIST:
            return hc_ref[idx, :, cs]
        return glu_ref[idx - CONV_HIST, :, cs]

    for t in range(DEC_SEQ):
        cols = []
        for c in range(D_CONV // 128):
            cs = slice(c * 128, (c + 1) * 128)
            acc = jnp.zeros((SB, 128), f32)
            for k in range(CONV_WIDTH):
                acc = acc + conv_row(t + k, cs) * wdw_ref[k:k + 1, cs]
            cols.append(acc + bdw_ref[:, cs])
        yn = _layer_norm(jnp.concatenate(cols, axis=-1), clg_ref[...], clb_ref[...])
        o_ref[t, :, C1:C2] = (yn * _sigmoid(yn)).astype(bf16)

    for t in range(DEC_SEQ):
        v_ref[t] = _layer_norm(jax.nn.gelu(p_ref[t, :, C4:P_IN]), slg_ref[...], slb_ref[...])
    for t in range(DEC_SEQ):
        for h in range(N_HEADS):
            hs = slice(h * 128, (h + 1) * 128)
            s = jnp.full((SB, 128), bs_ref[h * DEC_SEQ + t], f32)
            for k in range(t + 1):
                s = s + ws_ref[(h * DEC_SEQ + t) * DEC_SEQ + k] * v_ref[k, :, hs]
            u = jax.nn.gelu(p_ref[t, :, C3 + h * 128:C3 + (h + 1) * 128])
            o_ref[t, :, C2 + h * 128:C2 + (h + 1) * 128] = (u * s).astype(bf16)


def _mixs_call(l, ws_small, bs_small, proj_t, hp_t, hc_t, w_pool, pool_scale, w_dw, b_dw, clg, clb, slg, slb):
    c2 = lambda s: (l, 0, 0)
    return pl.pallas_call(
        _mixs_kernel,
        grid=(DEC_BATCH // SB,),
        in_specs=[
            pl.BlockSpec(memory_space=pltpu.SMEM),
            pl.BlockSpec(memory_space=pltpu.SMEM),
            pl.BlockSpec((DEC_SEQ, SB, P_IN), lambda s: (0, s, 0)),
            pl.BlockSpec((None, POOL_HIST, SB, D_POOL), lambda s: (l, 0, s, 0)),
            pl.BlockSpec((None, CONV_HIST, SB, D_CONV), lambda s: (l, 0, s, 0)),
            pl.BlockSpec((None, 4, POOL_GROUP, POOL_GROUP), lambda s: (l, 0, 0, 0)),
            pl.BlockSpec((None, 1, D_POOL), c2),
            pl.BlockSpec((None, CONV_WIDTH, D_CONV), c2),
            pl.BlockSpec((None, 1, D_CONV), c2),
            pl.BlockSpec((None, 1, D_CONV), c2),
            pl.BlockSpec((None, 1, D_CONV), c2),
            pl.BlockSpec((None, 1, D_CHUNK), c2),
            pl.BlockSpec((None, 1, D_CHUNK), c2),
        ],
        out_specs=[
            pl.BlockSpec((DEC_SEQ, SB, D), lambda s: (0, s, 0)),
            pl.BlockSpec((DEC_SEQ, SB, D_CONV), lambda s: (0, s, 0)),
            pl.BlockSpec((DEC_SEQ, SB, D_CHUNK), lambda s: (0, s, 0)),
        ],
        out_shape=[
            jax.ShapeDtypeStruct((DEC_SEQ, DEC_BATCH, D), bf16),
            jax.ShapeDtypeStruct((DEC_SEQ, DEC_BATCH, D_CONV), f32),
            jax.ShapeDtypeStruct((DEC_SEQ, DEC_BATCH, D_CHUNK), f32),
        ],
        compiler_params=_cparams(("arbitrary",)),
        name="mix_sample",
    )(ws_small, bs_small, proj_t, hp_t, hc_t, w_pool, pool_scale.reshape(DEPTH, 1, D_POOL), w_dw,
      b_dw.reshape(DEPTH, 1, D_CONV), clg.reshape(DEPTH, 1, D_CONV), clb.reshape(DEPTH, 1, D_CONV),
      slg.reshape(DEPTH, 1, D_CHUNK), slb.reshape(DEPTH, 1, D_CHUNK))


def _outproj_kernel(m_ref, w_ref, x_ref, gtp, gts, shp, shs, scp, scs, g_ref, x1_ref, h2_ref, wbf):
    i = pl.program_id(0)

    @pl.when(i == 0)
    def _():
        def cast_rows(r, carry):
            rs = pl.ds(pl.multiple_of(r * 256, 256), 256)
            wbf[rs, :] = w_ref[rs, :].astype(bf16)
            return carry
        lax.fori_loop(0, D // 256, cast_rows, 0)

    mix = jnp.dot(m_ref[...], wbf[...], preferred_element_type=f32)
    x1 = x_ref[...] + _sel_mod(i, TM_OUT, gtp, gts) * mix
    x1_ref[...] = x1
    h2 = _rms(x1, g_ref[...]) * (1.0 + _sel_mod(i, TM_OUT, scp, scs)) + _sel_mod(i, TM_OUT, shp, shs)
    h2_ref[...] = h2.astype(h2_ref.dtype)


def _outproj_call(l, mixcat, w_out, x, modp, mods, g_ffn, h2_dtype):
    gtp, gts = _mod_specs(l, 2, TM_OUT, 1)
    shp, shs = _mod_specs(l, 3, TM_OUT, 1)
    scp, scs = _mod_specs(l, 4, TM_OUT, 1)
    return pl.pallas_call(
        _outproj_kernel,
        grid=(N_TOK // TM_OUT,),
        in_specs=[
            pl.BlockSpec((TM_OUT, D), lambda i: (i, 0)),
            pl.BlockSpec((None, D, D), lambda i: (l, 0, 0), pipeline_mode=pl.Buffered(1)),
            pl.BlockSpec((TM_OUT, D), lambda i: (i, 0)),
            gtp, gts, shp, shs, scp, scs,
            pl.BlockSpec((None, 1, D), lambda i: (l, 0, 0)),
        ],
        out_specs=[
            pl.BlockSpec((TM_OUT, D), lambda i: (i, 0)),
            pl.BlockSpec((TM_OUT, D), lambda i: (i, 0)),
        ],
        out_shape=[
            jax.ShapeDtypeStruct((N_TOK, D), f32),
            jax.ShapeDtypeStruct((N_TOK, D), h2_dtype),
        ],
        scratch_shapes=[pltpu.VMEM((D, D), bf16)],
        compiler_params=_cparams(("arbitrary",)),
        name="out_proj",
    )(mixcat, w_out, x, modp, mods, modp, mods, modp, mods, g_ffn.reshape(DEPTH, 1, D))


def _group_ffn_kernel(ge_ref, gn_ref, rs_ref, tail_ref, x_hbm, wg_ref, wu_ref, wd_ref, y_hbm,
                      xbf, yacc, xstage, wgs, wus, wds, sem, *, sub, nsub, nj, all_full, chunk):
    del ge_ref
    g = pl.program_id(0)
    j = pl.program_id(1)
    n = gn_ref[g]

    @pl.when(jnp.logical_and(g == 0, j == 0))
    def _():
        yacc[0] = jnp.zeros((sub, D), f32)

        def tail_copy(c):
            r0 = pl.multiple_of(tail_ref[0] + c * sub, 8)
            return pltpu.make_async_copy(yacc.at[0], y_hbm.at[pl.ds(r0, sub), :], sem.at[1, 0])

        def start(c, carry):
            tail_copy(c).start()
            return carry

        def wait(c, carry):
            tail_copy(c).wait()
            return carry

        lax.fori_loop(0, tail_ref[1], start, 0)
        lax.fori_loop(0, tail_ref[1], wait, 0)

    def x_copy(m):
        r0 = pl.multiple_of(rs_ref[g * nsub + m], 8)
        return pltpu.make_async_copy(x_hbm.at[pl.ds(r0, sub), :], xstage.at[m % 2], sem.at[0, m % 2])

    def y_copy(grp, m):
        r0 = pl.multiple_of(rs_ref[grp * nsub + m], 8)
        return pltpu.make_async_copy(yacc.at[m], y_hbm.at[pl.ds(r0, sub), :], sem.at[1, m])

    def for_each_subtile(count, fn):
        def body(m, carry):
            fn(m)
            return carry
        lax.fori_loop(0, count, body, 0)

    @pl.when(jnp.logical_and(n > 0, j == 0))
    def _():
        x_copy(0).start()

        def land(m):
            @pl.when(m + 1 < n)
            def _():
                x_copy(m + 1).start()
            x_copy(m).wait()
            xbf[m] = xstage[m % 2].astype(bf16)
        for_each_subtile(n, land)

        @pl.when(g > 0)
        def _():
            for_each_subtile(gn_ref[jnp.maximum(g - 1, 0)], lambda m: y_copy(g - 1, m).wait())

        def clear(m):
            yacc[m] = jnp.zeros((sub, D), f32)
        for_each_subtile(n, clear)

    def swiglu_rows(m0, k, wg, wu, wd):
        x = xbf[pl.ds(m0, k)].reshape(k * sub, D)
        gate = jnp.dot(x, wg, preferred_element_type=f32)
        up = jnp.dot(x, wu, preferred_element_type=f32)
        act = (gate * _sigmoid(gate) * up).astype(bf16)
        yacc[pl.ds(m0, k)] += jnp.dot(act, wd, preferred_element_type=f32).reshape(k, sub, D)

    def narrow_weights_and_first(k):
        wg = wg_ref[...].astype(bf16)
        wu = wu_ref[...].astype(bf16)
        wd = wd_ref[...].astype(bf16)
        wgs[...] = wg
        wus[...] = wu
        wds[...] = wd
        swiglu_rows(0, k, wg, wu, wd)

    if all_full:
        @pl.when(n > 0)
        def _():
            narrow_weights_and_first(1)
            for m in range(1, nsub):
                swiglu_rows(m, 1, wgs[...], wus[...], wds[...])
    else:
        n_wide = n // chunk
        n_tail = n - chunk * n_wide

        @pl.when(n_wide > 0)
        def _():
            narrow_weights_and_first(chunk)

            def wide(t, carry):
                swiglu_rows(t * chunk, chunk, wgs[...], wus[...], wds[...])
                return carry
            lax.fori_loop(1, n_wide, wide, 0)
            for k in range(1, chunk):
                @pl.when(n_tail == k)
                def _():
                    swiglu_rows(n_wide * chunk, k, wgs[...], wus[...], wds[...])

        for k in range(1, chunk):
            @pl.when(jnp.logical_and(n_wide == 0, n_tail == k))
            def _():
                narrow_weights_and_first(k)

    @pl.when(jnp.logical_and(n > 0, j == nj - 1))
    def _():
        for_each_subtile(n, lambda m: y_copy(g, m).start())
        last = g + 1 >= pl.num_programs(0)
        next_n = gn_ref[jnp.minimum(g + 1, pl.num_programs(0) - 1)]

        @pl.when(jnp.logical_or(last, next_n == 0))
        def _():
            for_each_subtile(n, lambda m: y_copy(g, m).wait())


def _group_ffn_call(name, layer, group_expert, group_n, row_start, tail, x, wg, wu, wd, out_rows, sub, nsub,
                    n_groups, all_full, chunk=1):
    d_ff = wg.shape[-1]
    nj = d_ff // TF_FFN

    def jmap(g, j, gn):
        return jnp.where(gn[g] > 0, j, nj - 1)

    return pl.pallas_call(
        functools.partial(_group_ffn_kernel, sub=sub, nsub=nsub, nj=nj, all_full=all_full, chunk=chunk),
        grid_spec=pltpu.PrefetchScalarGridSpec(
            num_scalar_prefetch=4,
            grid=(n_groups, nj),
            in_specs=[
                pl.BlockSpec(memory_space=pl.ANY),
                pl.BlockSpec((None, None, D, TF_FFN), lambda g, j, ge, gn, rs, tl: (layer, ge[g], 0, jmap(g, j, gn))),
                pl.BlockSpec((None, None, D, TF_FFN), lambda g, j, ge, gn, rs, tl: (layer, ge[g], 0, jmap(g, j, gn))),
                pl.BlockSpec((None, None, TF_FFN, D), lambda g, j, ge, gn, rs, tl: (layer, ge[g], jmap(g, j, gn), 0)),
            ],
            out_specs=pl.BlockSpec(memory_space=pl.ANY),
            scratch_shapes=[
                pltpu.VMEM((nsub, sub, D), bf16),
                pltpu.VMEM((nsub, sub, D), f32),
                pltpu.VMEM((2, sub, D), f32),
                pltpu.VMEM((D, TF_FFN), bf16),
                pltpu.VMEM((D, TF_FFN), bf16),
                pltpu.VMEM((TF_FFN, D), bf16),
                pltpu.SemaphoreType.DMA((2, nsub)),
            ],
        ),
        out_shape=jax.ShapeDtypeStruct((out_rows, D), f32),
        compiler_params=_cparams(("arbitrary", "arbitrary")),
        name=name,
    )(group_expert, group_n, row_start, tail, x, wg, wu, wd)


def _dense_ffn_call(l, h2, wg, wu, wd):
    n_groups = N_TOK // (SUB_DENSE * NSUB_DENSE)
    zeros = jnp.zeros((n_groups,), i32)
    return _group_ffn_call(
        "ffn_dense", l // 2, zeros, zeros + NSUB_DENSE, jnp.arange(n_groups * NSUB_DENSE, dtype=i32) * SUB_DENSE,
        jnp.zeros((2,), i32), h2, wg[:, None], wu[:, None], wd[:, None], N_TOK, SUB_DENSE, NSUB_DENSE, n_groups,
        True)


def _residual_out(i, tm, x2, ep, outs, final):
    if final:
        (g_ref,), (yp_ref, ys_ref) = ep, outs
        y = _rms(x2, g_ref[...])
        npt = N_PROMPT // tm

        @pl.when(i < npt)
        def _():
            yp_ref[...] = y

        @pl.when(i >= npt)
        def _():
            ys_ref[...] = y
    else:
        (shp, shs, scp, scs, g_ref), (x_ref, h_ref) = ep, outs
        x_ref[...] = x2
        h_ref[...] = _modulated_norm(i, tm, x2, g_ref, shp, shs, scp, scs)


def _residual_specs(l, tm, nargs, g_mix, g_final):
    if nargs == 1:
        tile = lambda i: (i, 0)
        const3 = lambda i: (l + 1, 0, 0)
        const2 = lambda i: (0, 0)
        pmap = lambda i: (jnp.minimum(i, N_PROMPT // tm - 1), 0)
        smap = lambda i: (jnp.maximum(i - N_PROMPT // tm, 0), 0)
    else:
        tile = lambda i, s: (i, 0)
        const3 = lambda i, s: (l + 1, 0, 0)
        const2 = lambda i, s: (0, 0)
        pmap = lambda i, s: (jnp.minimum(i, N_PROMPT // tm - 1), 0)
        smap = lambda i, s: (jnp.maximum(i - N_PROMPT // tm, 0), 0)
    if l == DEPTH - 1:
        return ([g_final.reshape(1, D)], [pl.BlockSpec((1, D), const2)],
                [pl.BlockSpec((tm, D), pmap), pl.BlockSpec((tm, D), smap)],
                [jax.ShapeDtypeStruct((N_PROMPT, D), f32), jax.ShapeDtypeStruct((N_SAMPLE, D), f32)])
    mod_nargs = 1 if nargs == 1 else 3
    return (None, [*_mod_specs(l + 1, 0, tm, mod_nargs), *_mod_specs(l + 1, 1, tm, mod_nargs),
                   pl.BlockSpec((None, 1, D), const3)],
            [pl.BlockSpec((tm, D), tile), pl.BlockSpec((tm, D), tile)],
            [jax.ShapeDtypeStruct((N_TOK, D), f32), jax.ShapeDtypeStruct((N_TOK, D), bf16)])


def _dense_residual_kernel(y_ref, x1_ref, gtp, gts, *rest, final):
    i = pl.program_id(0)
    n_ep = 1 if final else 5
    x2 = x1_ref[...] + _sel_mod(i, TM, gtp, gts) * y_ref[...]
    _residual_out(i, TM, x2, rest[:n_ep], rest[n_ep:], final)


def _dense_residual_call(l, y, x1, modp, mods, g_mix, g_final):
    final = l == DEPTH - 1
    gtp, gts = _mod_specs(l, 5, TM, 1)
    ep_ops, ep_specs, out_specs, out_shape = _residual_specs(l, TM, 1, g_mix, g_final)
    if ep_ops is None:
        ep_ops = [modp, mods, modp, mods, g_mix.reshape(DEPTH, 1, D)]
    return pl.pallas_call(
        functools.partial(_dense_residual_kernel, final=final),
        grid=(N_TOK // TM,),
        in_specs=[pl.BlockSpec((TM, D), lambda i: (i, 0)), pl.BlockSpec((TM, D), lambda i: (i, 0)), gtp, gts, *ep_specs],
        out_specs=out_specs,
        out_shape=out_shape,
        compiler_params=_cparams(("arbitrary",)),
        name="ffn_residual",
    )(y, x1, modp, mods, *ep_ops)


def _router_kernel(h_ref, wr_ref, route_ref, cnt_ref, carry):
    i = pl.program_id(0)

    @pl.when(i == 0)
    def _():
        carry[...] = jnp.zeros((8, 128), f32)

    h = h_ref[...]
    hh = h.astype(bf16)
    hl = (h - hh.astype(f32)).astype(bf16)
    w = wr_ref[...]
    wh = w.astype(bf16)
    wl = (w - wh.astype(f32)).astype(bf16)
    logits = (jnp.dot(hh, wh, preferred_element_type=f32) + jnp.dot(hl, wh, preferred_element_type=f32)
              + jnp.dot(hh, wl, preferred_element_type=f32))
    lane = lax.broadcasted_iota(i32, (TM, 128), 1)
    neg = jnp.float32(-jnp.inf)
    lg = jnp.where(lane < N_EXPERTS, logits, neg)
    m1 = jnp.max(lg, axis=-1, keepdims=True)
    i1 = jnp.min(jnp.where(lg == m1, lane, 128), axis=-1, keepdims=True)
    lg2 = jnp.where(lane == i1, neg, lg)
    m2 = jnp.max(lg2, axis=-1, keepdims=True)
    i2 = jnp.min(jnp.where(lg2 == m2, lane, 128), axis=-1, keepdims=True)
    e2 = jnp.exp(m2 - m1)
    g1 = 1.0 / (1.0 + e2)
    g2 = e2 / (1.0 + e2)

    hit1 = lane == i1
    hit2 = lane == i2
    cnt = (hit1 | hit2).astype(f32)
    before = (lax.broadcasted_iota(i32, (TM, TM), 0) > lax.broadcasted_iota(i32, (TM, TM), 1))
    prefix = jnp.dot(before.astype(bf16), cnt.astype(bf16), preferred_element_type=f32) + carry[0:1, :]
    r1 = jnp.sum(jnp.where(hit1, prefix, 0.0), axis=-1, keepdims=True)
    r2 = jnp.sum(jnp.where(hit2, prefix, 0.0), axis=-1, keepdims=True)
    total = carry[...] + jnp.sum(cnt, axis=0, keepdims=True)
    carry[...] = total
    cnt_ref[...] = total

    out = jnp.where(lane == 0, i1.astype(f32), 0.0)
    out = jnp.where(lane == 1, i2.astype(f32), out)
    out = jnp.where(lane == 2, g1, out)
    out = jnp.where(lane == 3, g2, out)
    out = jnp.where(lane == 4, r1, out)
    out = jnp.where(lane == 5, r2, out)
    route_ref[...] = out


def _router_call(h2, wr_pad):
    return pl.pallas_call(
        _router_kernel,
        grid=(N_TOK // TM,),
        in_specs=[
            pl.BlockSpec((TM, D), lambda i: (i, 0)),
            pl.BlockSpec((D, 128), lambda i: (0, 0)),
        ],
        out_specs=[
            pl.BlockSpec((TM, 128), lambda i: (i, 0)),
            pl.BlockSpec((8, 128), lambda i: (0, 0)),
        ],
        out_shape=[
            jax.ShapeDtypeStruct((N_TOK, 128), f32),
            jax.ShapeDtypeStruct((8, 128), f32),
        ],
        scratch_shapes=[pltpu.VMEM((8, 128), f32)],
        compiler_params=_cparams(("arbitrary",)),
        name="moe_router",
    )(h2, wr_pad)


def _dispatch_kernel(pos_ref, pad_ref, h_ref, xs_ref, zbuf, sem, zsem, tsem):
    i = pl.program_id(0)

    def zero_copy(r):
        return pltpu.make_async_copy(zbuf.at[pl.ds(0, 1), :], xs_ref.at[pl.ds(r, 1), :], zsem)

    def tail_copy(c):
        r0 = pl.multiple_of(pad_ref[2 * N_EXPERTS] + c * SUB_MOE, 8)
        return pltpu.make_async_copy(zbuf, xs_ref.at[pl.ds(r0, SUB_MOE), :], tsem)

    def for_each_pad_row(fn):
        for e in range(N_EXPERTS):
            def body(r, carry):
                fn(r)
                return carry
            lax.fori_loop(pad_ref[e], pad_ref[N_EXPERTS + e], body, 0)

    def for_each_tail_tile(fn):
        def body(c, carry):
            fn(c)
            return carry
        lax.fori_loop(0, pad_ref[2 * N_EXPERTS + 1], body, 0)

    @pl.when(i == 0)
    def _():
        zbuf[...] = jnp.zeros((SUB_MOE, D), f32)
        for_each_pad_row(lambda r: zero_copy(r).start())
        for_each_tail_tile(lambda c: tail_copy(c).start())

    def row_copy(r, slot):
        p = pos_ref[2 * (i * TM + r) + slot]
        return pltpu.make_async_copy(h_ref.at[pl.ds(r, 1), :], xs_ref.at[pl.ds(p, 1), :], sem)

    def issue(r, carry):
        row_copy(r, 0).start(priority=0)
        row_copy(r, 1).start(priority=1)
        return carry

    lax.fori_loop(0, TM, issue, 0, unroll=8)
    for _ in range(2):
        pltpu.make_async_copy(h_ref, xs_ref.at[pl.ds(0, TM), :], sem).wait()

    @pl.when(i == 0)
    def _():
        for_each_pad_row(lambda r: zero_copy(r).wait())
        for_each_tail_tile(lambda c: tail_copy(c).wait())


def _dispatch_call(pos_flat, pad_bounds, h2):
    return pl.pallas_call(
        _dispatch_kernel,
        grid_spec=pltpu.PrefetchScalarGridSpec(
            num_scalar_prefetch=2,
            grid=(N_TOK // TM,),
            in_specs=[pl.BlockSpec((TM, D), lambda i, pos, pad: (i, 0))],
            out_specs=pl.BlockSpec(memory_space=pl.ANY),
            scratch_shapes=[pltpu.VMEM((SUB_MOE, D), f32), pltpu.SemaphoreType.DMA(()),
                            pltpu.SemaphoreType.DMA(()), pltpu.SemaphoreType.DMA(())],
        ),
        out_shape=jax.ShapeDtypeStruct((P_ROWS, D), f32),
        compiler_params=_cparams(("arbitrary",)),
        name="moe_dispatch",
    )(pos_flat, pad_bounds, h2)


def _combine_kernel(pos_ref, y_ref, x1_ref, route_ref, gtp, gts, *rest, final):
    n_ep = 1 if final else 5
    ep, outs, (buf, sem) = rest[:n_ep], rest[n_ep:-2], rest[-2:]
    i = pl.program_id(0)
    nt = pl.num_programs(0)
    slot = i % 2

    def start_tile(t, s):
        def issue(r, carry):
            for k in range(2):
                p = pos_ref[2 * (t * TM_COMB + r) + k]
                pltpu.make_async_copy(y_ref.at[pl.ds(p, 1), :], buf.at[s, k, pl.ds(r, 1), :],
                                      sem.at[s]).start(priority=k)
            return carry
        lax.fori_loop(0, TM_COMB, issue, 0, unroll=8)

    @pl.when(i == 0)
    def _():
        start_tile(0, 0)

    @pl.when(i + 1 < nt)
    def _():
        start_tile(i + 1, 1 - slot)

    for k in range(2):
        pltpu.make_async_copy(y_ref.at[pl.ds(0, TM_COMB), :], buf.at[slot, k], sem.at[slot]).wait()
    f = route_ref[:, 2:3] * buf[slot, 0] + route_ref[:, 3:4] * buf[slot, 1]
    x2 = x1_ref[...] + _sel_mod(i, TM_COMB, gtp, gts) * f
    _residual_out(i, TM_COMB, x2, ep, outs, final)


def _combine_call(l, pos_flat, y, x1, route, modp, mods, g_mix, g_final):
    final = l == DEPTH - 1
    gtp, gts = _mod_specs(l, 5, TM_COMB, 3)
    ep_ops, ep_specs, out_specs, out_shape = _residual_specs(l, TM_COMB, 2, g_mix, g_final)
    if ep_ops is None:
        ep_ops = [modp, mods, modp, mods, g_mix.reshape(DEPTH, 1, D)]
    return pl.pallas_call(
        functools.partial(_combine_kernel, final=final),
        grid_spec=pltpu.PrefetchScalarGridSpec(
            num_scalar_prefetch=1,
            grid=(N_TOK // TM_COMB,),
            in_specs=[
                pl.BlockSpec(memory_space=pl.ANY),
                pl.BlockSpec((TM_COMB, D), lambda i, pos: (i, 0)),
                pl.BlockSpec((TM_COMB, 128), lambda i, pos: (i, 0)),
                gtp, gts, *ep_specs,
            ],
            out_specs=out_specs,
            scratch_shapes=[
                pltpu.VMEM((2, 2, TM_COMB, D), f32),
                pltpu.SemaphoreType.DMA((2,)),
            ],
        ),
        out_shape=out_shape,
        compiler_params=_cparams(("arbitrary",)),
        name="moe_combine",
    )(pos_flat, y, x1, route, modp, mods, *ep_ops)


def _moe_layer(l, h2, x1, w_router, wg, wu, wd, modp, mods, g_mix, g_final):
    lm = l // 2
    wr_pad = jnp.pad(w_router[lm], ((0, 0), (0, 128 - N_EXPERTS)))
    route, cnt = _router_call(h2, wr_pad)
    expert = route[:, 0:2].astype(i32)
    rank = route[:, 4:6].astype(i32)
    counts = cnt[0, :N_EXPERTS].astype(i32)
    tiles = (counts + SUB_MOE - 1) // SUB_MOE
    tile_start = jnp.cumsum(tiles) - tiles
    row_start = tile_start * SUB_MOE
    pos_flat = (row_start[expert] + rank).reshape(-1)
    n_tiles = jnp.sum(tiles)
    tail = jnp.stack([n_tiles * SUB_MOE, MAX_TILES - n_tiles])
    pad_bounds = jnp.concatenate([row_start + counts, row_start + tiles * SUB_MOE, tail])
    groups = (tiles + NSUB_MOE - 1) // NSUB_MOE
    group_end = jnp.cumsum(groups)
    n_groups = group_end[-1]
    gid = jnp.arange(MAX_GROUPS, dtype=i32)
    gcl = jnp.minimum(gid, n_groups - 1)
    group_expert = jnp.sum((gcl[:, None] >= group_end[None, :]).astype(i32), axis=1)
    per_group = tiles // jnp.maximum(groups, 1)
    extra = tiles - per_group * groups
    k = gcl - (group_end - groups)[group_expert]
    first = k * per_group[group_expert] + jnp.minimum(k, extra[group_expert])
    group_n = jnp.where(gid < n_groups, per_group[group_expert] + (k < extra[group_expert]).astype(i32), 0)
    sub_start = (tile_start[group_expert] + first)[:, None] + jnp.arange(NSUB_MOE, dtype=i32)[None, :]
    xs = _dispatch_call(pos_flat, pad_bounds, h2)
    y = _group_ffn_call("moe_experts", lm, group_expert, group_n, (sub_start * SUB_MOE).reshape(-1), tail, xs,
                        wg, wu, wd, P_ROWS, SUB_MOE, NSUB_MOE, MAX_GROUPS, False, CHUNK_MOE)
    return _combine_call(l, pos_flat, y, x1, route, modp, mods, g_mix, g_final)


def kernel(x_prompt, x_sample, state_pool, state_conv, c_prompt, c_sample, w_ada, b_ada, g_mix, g_ffn, w_in, w_pool, pool_scale, w_dw, b_dw, conv_ln_g, conv_ln_b, sgu_ln_g, sgu_ln_b, w_spatial, b_spatial, w_out, w_ff_gate, w_ff_up, w_ff_down, w_router, w_exp_gate, w_exp_up, w_exp_down, g_final):
    c_all = jnp.concatenate([c_sample, c_prompt, jnp.zeros((N_SEQ_PAD - N_SAMPLE - BATCH, D), f32)], axis=0)
    modp = mods = _ada_call(c_all, w_ada, b_ada)
    hp_t = state_pool.transpose(0, 2, 1, 3)
    hc_t = state_conv.transpose(0, 2, 1, 3)
    bs_bcast = jnp.broadcast_to(b_spatial[:, :, :, None], (DEPTH, N_HEADS, CHUNK, CHUNK))
    ws_small = w_spatial[:, :, :DEC_SEQ, :DEC_SEQ].reshape(DEPTH, -1)
    bs_small = b_spatial[:, :, :DEC_SEQ].reshape(DEPTH, -1)
    w_in_bf = w_in.astype(bf16)

    x, h1 = _prenorm_call(x_prompt, x_sample, modp, mods, g_mix)
    pool_p, conv_p, pool_s, conv_s, chunk_v = [], [], [], [], []
    for l in range(DEPTH):
        proj = _inproj_call(l, h1, w_in_bf)
        proj_s = proj[N_PROMPT:].reshape(DEC_BATCH, DEC_SEQ, P_IN)
        mix_s, glu_s, v_s = _mixs_call(l, ws_small[l], bs_small[l], proj_s.transpose(1, 0, 2), hp_t, hc_t,
                                       w_pool, pool_scale, w_dw, b_dw, conv_ln_g, conv_ln_b, sgu_ln_g, sgu_ln_b)
        mixcat, npool, nconv = _mixp_call(l, proj, mix_s.transpose(1, 0, 2).reshape(N_SAMPLE, D), w_pool, pool_scale,
                                          w_dw, b_dw, conv_ln_g, conv_ln_b, sgu_ln_g, sgu_ln_b, w_spatial, bs_bcast)
        pool_p.append(npool[:, POOL_HALO - POOL_HIST:])
        conv_p.append(nconv[:, CONV_HALO - CONV_HIST:])
        pool_s.append(jnp.concatenate([state_pool[l][:, DEC_SEQ:], proj_s[:, :, :D_POOL]], axis=1))
        conv_s.append(jnp.concatenate([state_conv[l][:, DEC_SEQ:], glu_s.transpose(1, 0, 2)], axis=1))
        chunk_v.append(v_s.transpose(1, 0, 2))
        x1, h2 = _outproj_call(l, mixcat, w_out, x, modp, mods, g_ffn, f32)
        if l % 2 == 0:
            y = _dense_ffn_call(l, h2, w_ff_gate, w_ff_up, w_ff_down)
            out = _dense_residual_call(l, y, x1, modp, mods, g_mix, g_final)
        else:
            out = _moe_layer(l, h2, x1, w_router, w_exp_gate, w_exp_up, w_exp_down, modp, mods, g_mix, g_final)
        if l < DEPTH - 1:
            x, h1 = out
    y_prompt, y_sample = out
    return (y_prompt.reshape(BATCH, SEQ, D), y_sample.reshape(DEC_BATCH, DEC_SEQ, D),
            jnp.stack(pool_p), jnp.stack(conv_p), jnp.stack(pool_s), jnp.stack(conv_s), jnp.stack(chunk_v))
```

```python
import functools

import jax
import jax.numpy as jnp
from jax import lax
from jax.experimental import pallas as pl
from jax.experimental.pallas import tpu as pltpu

f32 = jnp.float32
bf16 = jnp.bfloat16
i32 = jnp.int32

D = 2048
BATCH = 4
SEQ = 2048
DEPTH = 4
DEC_BATCH = 128
DEC_SEQ = 4
PAST_LEN = 16384
D_POOL = 512
POOL_WINDOWS = (2, 4, 8, 16)
POOL_GROUP = 128
POOL_HIST = 15
D_CONV = 768
CONV_WIDTH = 31
CONV_HIST = 30
D_CHUNK = 768
CHUNK = 128
N_HEADS = 6
P_IN = 3584
D_FF = 5632
N_EXPERTS = 8
D_FF_EXPERT = 7168
EPS = 1e-6

N_PROMPT = BATCH * SEQ
N_SAMPLE = DEC_BATCH * DEC_SEQ
N_TOK = N_PROMPT + N_SAMPLE
N_SEQ = BATCH + DEC_BATCH
N_SEQ_PAD = N_SAMPLE + 8

C1 = D_POOL
C2 = C1 + D_CONV
C3 = C2 + D_CONV
C4 = C3 + D_CHUNK

VMEM_LIMIT = 56 * 1024 * 1024
TM = 512
TM_OUT = 256
TN_ADA = 1024
TT = 512
ROW_CHUNK = 64
POOL_HALO = 16
CONV_HALO = 32
SB = 32
TF_FFN = 256
SUB_MOE = 128
NSUB_MOE = 22
CHUNK_MOE = 6
SUB_DENSE = N_TOK // 16
NSUB_DENSE = 4
TM_COMB = 256
MAX_TILES = (N_TOK * 2) // SUB_MOE + N_EXPERTS
MAX_GROUPS = (MAX_TILES + (NSUB_MOE - 1) * N_EXPERTS) // NSUB_MOE
P_ROWS = MAX_TILES * SUB_MOE


def _cparams(sem):
    return pltpu.CompilerParams(dimension_semantics=sem, vmem_limit_bytes=VMEM_LIMIT)


def _sigmoid(x):
    return jax.nn.sigmoid(x)


def _rms(x, g):
    return x * lax.rsqrt(jnp.mean(x * x, axis=-1, keepdims=True) + EPS) * g


def _layer_norm(y, g, b):
    mu = jnp.mean(y, axis=-1, keepdims=True)
    d = y - mu
    var = jnp.mean(d * d, axis=-1, keepdims=True)
    return d * lax.rsqrt(var + EPS) * g + b


def _sel_mod(i, tm, p_ref, s_ref):
    seq = jnp.minimum(i // (SEQ // tm), BATCH - 1)
    return jnp.where(i >= N_PROMPT // tm, s_ref[...], p_ref[pl.ds(seq, 1), :])


def _mod_specs(l, chunk, tm, nargs):
    npt = N_PROMPT // tm
    pblk = N_SAMPLE // 8
    if nargs == 1:
        mp = lambda i: (l, pblk, chunk)
        ms = lambda i: (l, jnp.maximum(i - npt, 0), chunk)
    elif nargs == 2:
        mp = lambda i, j: (l, pblk, chunk)
        ms = lambda i, j: (l, jnp.maximum(i - npt, 0), chunk)
    else:
        mp = lambda i, s: (l, pblk, chunk)
        ms = lambda i, s: (l, jnp.maximum(i - npt, 0), chunk)
    return pl.BlockSpec((None, 8, D), mp), pl.BlockSpec((None, tm, D), ms, pipeline_mode=pl.Buffered(1))


def _ada_kernel(c_ref, w_ref, b_ref, o_ref):
    c = c_ref[...]
    s = (c * _sigmoid(c)).astype(bf16)
    mod = jnp.dot(s, w_ref[...].astype(bf16), preferred_element_type=f32) + b_ref[...]
    seq_mod = mod[0:DEC_BATCH]
    hi = seq_mod.astype(bf16)
    rest = seq_mod - hi.astype(f32)
    mid = rest.astype(bf16)
    lo = (rest - mid.astype(f32)).astype(bf16)
    token = lax.broadcasted_iota(i32, (N_SAMPLE, DEC_BATCH), 0)
    owner = lax.broadcasted_iota(i32, (N_SAMPLE, DEC_BATCH), 1)
    rep = (token // DEC_SEQ == owner).astype(bf16)
    expand = lambda part: jnp.dot(rep, part, preferred_element_type=f32)
    o_ref[0:N_SAMPLE, :] = (expand(hi) + expand(mid)) + expand(lo)
    o_ref[N_SAMPLE:N_SEQ_PAD, :] = mod[DEC_BATCH:DEC_BATCH + N_SEQ_PAD - N_SAMPLE]


def _ada_call(c_all, w_ada, b_ada):
    nj = 6 * D // TN_ADA
    return pl.pallas_call(
        _ada_kernel,
        grid=(DEPTH, nj),
        in_specs=[
            pl.BlockSpec((DEC_BATCH + N_SEQ_PAD - N_SAMPLE, D), lambda l, j: (0, 0)),
            pl.BlockSpec((None, D, TN_ADA), lambda l, j: (l, 0, j)),
            pl.BlockSpec((None, 1, TN_ADA), lambda l, j: (l, 0, j)),
        ],
        out_specs=pl.BlockSpec((None, N_SEQ_PAD, TN_ADA), lambda l, j: (l, 0, j)),
        out_shape=jax.ShapeDtypeStruct((DEPTH, N_SEQ_PAD, 6 * D), f32),
        compiler_params=_cparams(("arbitrary", "arbitrary")),
        name="ada_mod",
    )(c_all, w_ada, b_ada.reshape(DEPTH, 1, 6 * D))


def _modulated_norm(i, tm, x, g_ref, shp, shs, scp, scs):
    h = _rms(x, g_ref[...]) * (1.0 + _sel_mod(i, tm, scp, scs)) + _sel_mod(i, tm, shp, shs)
    return h.astype(bf16)


def _prenorm_kernel(xp_ref, xs_ref, shp, shs, scp, scs, g_ref, x_ref, h_ref):
    i = pl.program_id(0)
    x = jnp.where(i >= N_PROMPT // TM, xs_ref[...], xp_ref[...])
    x_ref[...] = x
    h_ref[...] = _modulated_norm(i, TM, x, g_ref, shp, shs, scp, scs)


def _prenorm_call(x_prompt, x_sample, modp, mods, g_mix):
    assert N_SAMPLE == TM
    shp, shs = _mod_specs(0, 0, TM, 1)
    scp, scs = _mod_specs(0, 1, TM, 1)
    npt = N_PROMPT // TM
    return pl.pallas_call(
        _prenorm_kernel,
        grid=(N_TOK // TM,),
        in_specs=[
            pl.BlockSpec((TM, D), lambda i: (jnp.minimum(i, npt - 1), 0)),
            pl.BlockSpec((TM, D), lambda i: (0, 0), pipeline_mode=pl.Buffered(1)),
            shp, shs, scp, scs,
            pl.BlockSpec((None, 1, D), lambda i: (0, 0, 0)),
        ],
        out_specs=[pl.BlockSpec((TM, D), lambda i: (i, 0)), pl.BlockSpec((TM, D), lambda i: (i, 0))],
        out_shape=[jax.ShapeDtypeStruct((N_TOK, D), f32), jax.ShapeDtypeStruct((N_TOK, D), bf16)],
        compiler_params=_cparams(("arbitrary",)),
        name="prenorm",
    )(x_prompt.reshape(N_PROMPT, D), x_sample.reshape(N_SAMPLE, D), modp, mods, modp, mods,
      g_mix.reshape(DEPTH, 1, D))


def _inproj_kernel(h_ref, w_ref, o_ref):
    o_ref[...] = jnp.dot(h_ref[...], w_ref[...], preferred_element_type=f32)


def _inproj_call(l, h1, w_in_bf):
    return pl.pallas_call(
        _inproj_kernel,
        grid=(N_TOK // TM,),
        in_specs=[
            pl.BlockSpec((TM, D), lambda i: (i, 0)),
            pl.BlockSpec((None, D, P_IN), lambda i: (l, 0, 0), pipeline_mode=pl.Buffered(1)),
        ],
        out_specs=pl.BlockSpec((TM, P_IN), lambda i: (i, 0)),
        out_shape=jax.ShapeDtypeStruct((N_TOK, P_IN), f32),
        compiler_params=_cparams(("arbitrary",)),
        name="in_proj",
    )(h1, w_in_bf)


def _mixp_kernel(p_ref, ms_ref, wpool_ref, pscale_ref, wdw_ref, bdw_ref, clg_ref, clb_ref, slg_ref, slb_ref,
                 ws_ref, bs_ref, o_ref, *state_and_scratch):
    b = pl.program_id(0)

    @pl.when(b < BATCH)
    def _():
        _mixp_body(p_ref, wpool_ref, pscale_ref, wdw_ref, bdw_ref, clg_ref, clb_ref, slg_ref, slb_ref,
                   ws_ref, bs_ref, o_ref, *state_and_scratch)

    @pl.when(jnp.logical_and(b == BATCH, pl.program_id(1) < N_SAMPLE // TT))
    def _():
        o_ref[...] = ms_ref[...]


def _mixp_body(p_ref, wpool_ref, pscale_ref, wdw_ref, bdw_ref, clg_ref, clb_ref, slg_ref, slb_ref,
               ws_ref, bs_ref, o_ref, npool_ref, nconv_ref, pext, cext, ybuf):
    t = pl.program_id(1)

    @pl.when(t == 0)
    def _():
        pext[0:POOL_HALO, :] = jnp.zeros((POOL_HALO, D_POOL), f32)
        cext[0:CONV_HALO, :] = jnp.zeros((CONV_HALO, D_CONV), f32)

    pext[POOL_HALO:POOL_HALO + TT, :] = p_ref[:, 0:C1]
    pos = lax.broadcasted_iota(i32, (TT, POOL_GROUP), 0) + t * TT
    for g, w in enumerate(POOL_WINDOWS):
        sl = slice(g * POOL_GROUP, (g + 1) * POOL_GROUP)
        tok = pext[POOL_HALO:POOL_HALO + TT, sl]
        s = tok
        for j in range(1, w):
            s = s + pext[POOL_HALO - j:POOL_HALO - j + TT, sl]
        cnt = jnp.minimum(pos + 1, w).astype(f32)
        diff = (s / cnt - tok).astype(bf16)
        o = jnp.dot(diff, wpool_ref[g].astype(bf16), preferred_element_type=f32) * pscale_ref[:, sl]
        o_ref[:, sl] = o.astype(bf16)
    tail = pext[TT:TT + POOL_HALO, :]
    npool_ref[...] = tail
    pext[0:POOL_HALO, :] = tail

    cext[CONV_HALO:CONV_HALO + TT, :] = p_ref[:, C1:C2] * _sigmoid(p_ref[:, C2:C3])
    off = CONV_HALO - CONV_HIST
    for c in range(D_CONV // 128):
        cs = slice(c * 128, (c + 1) * 128)
        for r in range(TT // ROW_CHUNK):
            r0 = r * ROW_CHUNK
            acc = bdw_ref[:, cs]
            for res in range(8):
                rows = ROW_CHUNK if res == 0 else ROW_CHUNK + 8
                z = None
                for a in range((CONV_WIDTH + off) // 8 + 1):
                    k = 8 * a + res - off
                    if 0 <= k < CONV_WIDTH:
                        term = cext[r0 + 8 * a:r0 + 8 * a + rows, cs] * wdw_ref[k:k + 1, cs]
                        z = term if z is None else z + term
                acc = acc + (z if res == 0 else pltpu.roll(z, rows - res, axis=0)[0:ROW_CHUNK])
            ybuf[r0:r0 + ROW_CHUNK, cs] = acc
    yn = _layer_norm(ybuf[...], clg_ref[...], clb_ref[...])
    o_ref[:, C1:C2] = (yn * _sigmoid(yn)).astype(bf16)
    ctail = cext[TT:TT + CONV_HALO, :]
    nconv_ref[...] = ctail
    cext[0:CONV_HALO, :] = ctail

    ybuf[...] = _layer_norm(jax.nn.gelu(p_ref[:, C4:P_IN]), slg_ref[...], slb_ref[...])
    causal = (lax.broadcasted_iota(i32, (CHUNK, CHUNK), 0) >= lax.broadcasted_iota(i32, (CHUNK, CHUNK), 1))
    for h in range(N_HEADS):
        hs = slice(h * 128, (h + 1) * 128)
        wm = jnp.where(causal, ws_ref[h], 0.0).astype(bf16)
        for n in range(TT // CHUNK):
            rs = slice(n * CHUNK, (n + 1) * CHUNK)
            s = jnp.dot(wm, ybuf[rs, hs].astype(bf16), preferred_element_type=f32) + bs_ref[h]
            u = jax.nn.gelu(p_ref[rs, C3 + h * 128:C3 + (h + 1) * 128])
            o_ref[rs, C2 + h * 128:C2 + (h + 1) * 128] = (u * s).astype(bf16)


def _mixp_call(l, proj, mix_sample, w_pool, pool_scale, w_dw, b_dw, clg, clb, slg, slb, w_spatial, bs_bcast):
    nt = SEQ // TT
    last = N_TOK // TT - 1
    c2 = lambda b, t: (l, 0, 0)
    tile = lambda b, t: (jnp.minimum(b * nt + t, last), 0)
    seq = lambda b, t: (jnp.minimum(b, BATCH - 1), 0, 0)
    return pl.pallas_call(
        _mixp_kernel,
        grid=(BATCH + 1, nt),
        in_specs=[
            pl.BlockSpec((TT, P_IN), tile),
            pl.BlockSpec((TT, D), lambda b, t: (jnp.where(b == BATCH, jnp.minimum(t, N_SAMPLE // TT - 1), 0), 0)),
            pl.BlockSpec((None, 4, POOL_GROUP, POOL_GROUP), lambda b, t: (l, 0, 0, 0)),
            pl.BlockSpec((None, 1, D_POOL), c2),
            pl.BlockSpec((None, CONV_WIDTH, D_CONV), c2),
            pl.BlockSpec((None, 1, D_CONV), c2),
            pl.BlockSpec((None, 1, D_CONV), c2),
            pl.BlockSpec((None, 1, D_CONV), c2),
            pl.BlockSpec((None, 1, D_CHUNK), c2),
            pl.BlockSpec((None, 1, D_CHUNK), c2),
            pl.BlockSpec((None, N_HEADS, CHUNK, CHUNK), lambda b, t: (l, 0, 0, 0)),
            pl.BlockSpec((None, N_HEADS, CHUNK, CHUNK), lambda b, t: (l, 0, 0, 0)),
        ],
        out_specs=[
            pl.BlockSpec((TT, D), tile),
            pl.BlockSpec((None, POOL_HALO, D_POOL), seq),
            pl.BlockSpec((None, CONV_HALO, D_CONV), seq),
        ],
        out_shape=[
            jax.ShapeDtypeStruct((N_TOK, D), bf16),
            jax.ShapeDtypeStruct((BATCH, POOL_HALO, D_POOL), f32),
            jax.ShapeDtypeStruct((BATCH, CONV_HALO, D_CONV), f32),
        ],
        scratch_shapes=[
            pltpu.VMEM((TT + POOL_HALO, D_POOL), f32),
            pltpu.VMEM((TT + CONV_HALO, D_CONV), f32),
            pltpu.VMEM((TT, D_CONV), f32),
        ],
        compiler_params=_cparams(("arbitrary", "arbitrary")),
        name="mix_prompt",
    )(proj, mix_sample, w_pool, pool_scale.reshape(DEPTH, 1, D_POOL), w_dw, b_dw.reshape(DEPTH, 1, D_CONV),
      clg.reshape(DEPTH, 1, D_CONV), clb.reshape(DEPTH, 1, D_CONV), slg.reshape(DEPTH, 1, D_CHUNK),
      slb.reshape(DEPTH, 1, D_CHUNK), w_spatial, bs_bcast)


def _mixs_kernel(ws_ref, bs_ref, p_ref, hp_ref, hc_ref, wpool_ref, pscale_ref, wdw_ref, bdw_ref,
                 clg_ref, clb_ref, slg_ref, slb_ref, o_ref, glu_ref, v_ref):
    def pool_row(idx, sl):
        if idx < POOL_HIST:
            return hp_ref[idx, :, sl]
        return p_ref[idx - POOL_HIST, :, sl]

    for g, w in enumerate(POOL_WINDOWS):
        sl = slice(g * POOL_GROUP, (g + 1) * POOL_GROUP)
        diffs = []
        for t in range(DEC_SEQ):
            tok = pool_row(POOL_HIST + t, sl)
            s = tok
            for j in range(1, w):
                s = s + pool_row(POOL_HIST + t - j, sl)
            diffs.append(s * (1.0 / w) - tok)
        d = jnp.concatenate(diffs, axis=0).astype(bf16)
        o = jnp.dot(d, wpool_ref[g].astype(bf16), preferred_element_type=f32) * pscale_ref[:, sl]
        for t in range(DEC_SEQ):
            o_ref[t, :, sl] = o[t * SB:(t + 1) * SB].astype(bf16)

    for t in range(DEC_SEQ):
        glu_ref[t] = p_ref[t, :, C1:C2] * _sigmoid(p_ref[t, :, C2:C3])

    def conv_row(idx, cs):
        if idx < CONV_HIST:
            return hc_ref[idx, :, cs]
        return glu_ref[idx - CONV_HIST, :, cs]

    for t in range(DEC_SEQ):
        cols = []
        for c in range(D_CONV // 128):
            cs = slice(c * 128, (c + 1) * 128)
            acc = jnp.zeros((SB, 128), f32)
            for k in range(CONV_WIDTH):
                acc = acc + conv_row(t + k, cs) * wdw_ref[k:k + 1, cs]
            cols.append(acc + bdw_ref[:, cs])
        yn = _layer_norm(jnp.concatenate(cols, axis=-1), clg_ref[...], clb_ref[...])
        o_ref[t, :, C1:C2] = (yn * _sigmoid(yn)).astype(bf16)

    for t in range(DEC_SEQ):
        v_ref[t] = _layer_norm(jax.nn.gelu(p_ref[t, :, C4:P_IN]), slg_ref[...], slb_ref[...])
    for t in range(DEC_SEQ):
        for h in range(N_HEADS):
            hs = slice(h * 128, (h + 1) * 128)
            s = jnp.full((SB, 128), bs_ref[h * DEC_SEQ + t], f32)
            for k in range(t + 1):
                s = s + ws_ref[(h * DEC_SEQ + t) * DEC_SEQ + k] * v_ref[k, :, hs]
            u = jax.nn.gelu(p_ref[t, :, C3 + h * 128:C3 + (h + 1) * 128])
            o_ref[t, :, C2 + h * 128:C2 + (h + 1) * 128] = (u * s).astype(bf16)


def _mixs_call(l, ws_small, bs_small, proj_t, hp_t, hc_t, w_pool, pool_scale, w_dw, b_dw, clg, clb, slg, slb):
    c2 = lambda s: (l, 0, 0)
    return pl.pallas_call(
        _mixs_kernel,
        grid=(DEC_BATCH // SB,),
        in_specs=[
            pl.BlockSpec(memory_space=pltpu.SMEM),
            pl.BlockSpec(memory_space=pltpu.SMEM),
            pl.BlockSpec((DEC_SEQ, SB, P_IN), lambda s: (0, s, 0)),
            pl.BlockSpec((None, POOL_HIST, SB, D_POOL), lambda s: (l, 0, s, 0)),
            pl.BlockSpec((None, CONV_HIST, SB, D_CONV), lambda s: (l, 0, s, 0)),
            pl.BlockSpec((None, 4, POOL_GROUP, POOL_GROUP), lambda s: (l, 0, 0, 0)),
            pl.BlockSpec((None, 1, D_POOL), c2),
            pl.BlockSpec((None, CONV_WIDTH, D_CONV), c2),
            pl.BlockSpec((None, 1, D_CONV), c2),
            pl.BlockSpec((None, 1, D_CONV), c2),
            pl.BlockSpec((None, 1, D_CONV), c2),
            pl.BlockSpec((None, 1, D_CHUNK), c2),
            pl.BlockSpec((None, 1, D_CHUNK), c2),
        ],
        out_specs=[
            pl.BlockSpec((DEC_SEQ, SB, D), lambda s: (0, s, 0)),
            pl.BlockSpec((DEC_SEQ, SB, D_CONV), lambda s: (0, s, 0)),
            pl.BlockSpec((DEC_SEQ, SB, D_CHUNK), lambda s: (0, s, 0)),
        ],
        out_shape=[
            jax.ShapeDtypeStruct((DEC_SEQ, DEC_BATCH, D), bf16),
            jax.ShapeDtypeStruct((DEC_SEQ, DEC_BATCH, D_CONV), f32),
            jax.ShapeDtypeStruct((DEC_SEQ, DEC_BATCH, D_CHUNK), f32),
        ],
        compiler_params=_cparams(("arbitrary",)),
        name="mix_sample",
    )(ws_small, bs_small, proj_t, hp_t, hc_t, w_pool, pool_scale.reshape(DEPTH, 1, D_POOL), w_dw,
      b_dw.reshape(DEPTH, 1, D_CONV), clg.reshape(DEPTH, 1, D_CONV), clb.reshape(DEPTH, 1, D_CONV),
      slg.reshape(DEPTH, 1, D_CHUNK), slb.reshape(DEPTH, 1, D_CHUNK))


def _outproj_kernel(m_ref, w_ref, x_ref, gtp, gts, shp, shs, scp, scs, g_ref, x1_ref, h2_ref, wbf):
    i = pl.program_id(0)

    @pl.when(i == 0)
    def _():
        def cast_rows(r, carry):
            rs = pl.ds(pl.multiple_of(r * 256, 256), 256)
            wbf[rs, :] = w_ref[rs, :].astype(bf16)
            return carry
        lax.fori_loop(0, D // 256, cast_rows, 0)

    mix = jnp.dot(m_ref[...], wbf[...], preferred_element_type=f32)
    x1 = x_ref[...] + _sel_mod(i, TM_OUT, gtp, gts) * mix
    x1_ref[...] = x1
    h2 = _rms(x1, g_ref[...]) * (1.0 + _sel_mod(i, TM_OUT, scp, scs)) + _sel_mod(i, TM_OUT, shp, shs)
    h2_ref[...] = h2.astype(h2_ref.dtype)


def _outproj_call(l, mixcat, w_out, x, modp, mods, g_ffn, h2_dtype):
    gtp, gts = _mod_specs(l, 2, TM_OUT, 1)
    shp, shs = _mod_specs(l, 3, TM_OUT, 1)
    scp, scs = _mod_specs(l, 4, TM_OUT, 1)
    return pl.pallas_call(
        _outproj_kernel,
        grid=(N_TOK // TM_OUT,),
        in_specs=[
            pl.BlockSpec((TM_OUT, D), lambda i: (i, 0)),
            pl.BlockSpec((None, D, D), lambda i: (l, 0, 0), pipeline_mode=pl.Buffered(1)),
            pl.BlockSpec((TM_OUT, D), lambda i: (i, 0)),
            gtp, gts, shp, shs, scp, scs,
            pl.BlockSpec((None, 1, D), lambda i: (l, 0, 0)),
        ],
        out_specs=[
            pl.BlockSpec((TM_OUT, D), lambda i: (i, 0)),
            pl.BlockSpec((TM_OUT, D), lambda i: (i, 0)),
        ],
        out_shape=[
            jax.ShapeDtypeStruct((N_TOK, D), f32),
            jax.ShapeDtypeStruct((N_TOK, D), h2_dtype),
        ],
        scratch_shapes=[pltpu.VMEM((D, D), bf16)],
        compiler_params=_cparams(("arbitrary",)),
        name="out_proj",
    )(mixcat, w_out, x, modp, mods, modp, mods, modp, mods, g_ffn.reshape(DEPTH, 1, D))


def _group_ffn_kernel(ge_ref, gn_ref, rs_ref, tail_ref, x_hbm, wg_ref, wu_ref, wd_ref, y_hbm,
                      xbf, yacc, xstage, wgs, wus, wds, sem, *, sub, nsub, nj, all_full, chunk):
    del ge_ref
    g = pl.program_id(0)
    j = pl.program_id(1)
    n = gn_ref[g]

    @pl.when(jnp.logical_and(g == 0, j == 0))
    def _():
        yacc[0] = jnp.zeros((sub, D), f32)

        def tail_copy(c):
            r0 = pl.multiple_of(tail_ref[0] + c * sub, 8)
            return pltpu.make_async_copy(yacc.at[0], y_hbm.at[pl.ds(r0, sub), :], sem.at[1, 0])

        def start(c, carry):
            tail_copy(c).start()
            return carry

        def wait(c, carry):
            tail_copy(c).wait()
            return carry

        lax.fori_loop(0, tail_ref[1], start, 0)
        lax.fori_loop(0, tail_ref[1], wait, 0)

    def x_copy(m):
        r0 = pl.multiple_of(rs_ref[g * nsub + m], 8)
        return pltpu.make_async_copy(x_hbm.at[pl.ds(r0, sub), :], xstage.at[m % 2], sem.at[0, m % 2])

    def y_copy(grp, m):
        r0 = pl.multiple_of(rs_ref[grp * nsub + m], 8)
        return pltpu.make_async_copy(yacc.at[m], y_hbm.at[pl.ds(r0, sub), :], sem.at[1, m])

    def for_each_subtile(count, fn):
        def body(m, carry):
            fn(m)
            return carry
        lax.fori_loop(0, count, body, 0)

    @pl.when(jnp.logical_and(n > 0, j == 0))
    def _():
        x_copy(0).start()

        def land(m):
            @pl.when(m + 1 < n)
            def _():
                x_copy(m + 1).start()
            x_copy(m).wait()
            xbf[m] = xstage[m % 2].astype(bf16)
        for_each_subtile(n, land)

        @pl.when(g > 0)
        def _():
            for_each_subtile(gn_ref[jnp.maximum(g - 1, 0)], lambda m: y_copy(g - 1, m).wait())

        def clear(m):
            yacc[m] = jnp.zeros((sub, D), f32)
        for_each_subtile(n, clear)

    def swiglu_rows(m0, k, wg, wu, wd):
        x = xbf[pl.ds(m0, k)].reshape(k * sub, D)
        gate = jnp.dot(x, wg, preferred_element_type=f32)
        up = jnp.dot(x, wu, preferred_element_type=f32)
        act = (gate * _sigmoid(gate) * up).astype(bf16)
        yacc[pl.ds(m0, k)] += jnp.dot(act, wd, preferred_element_type=f32).reshape(k, sub, D)

    def narrow_weights_and_first(k):
        wg = wg_ref[...].astype(bf16)
        wu = wu_ref[...].astype(bf16)
        wd = wd_ref[...].astype(bf16)
        wgs[...] = wg
        wus[...] = wu
        wds[...] = wd
        swiglu_rows(0, k, wg, wu, wd)

    if all_full:
        @pl.when(n > 0)
        def _():
            narrow_weights_and_first(1)
            for m in range(1, nsub):
                swiglu_rows(m, 1, wgs[...], wus[...], wds[...])
    else:
        n_wide = n // chunk
        n_tail = n - chunk * n_wide

        @pl.when(n_wide > 0)
        def _():
            narrow_weights_and_first(chunk)

            def wide(t, carry):
                swiglu_rows(t * chunk, chunk, wgs[...], wus[...], wds[...])
                return carry
            lax.fori_loop(1, n_wide, wide, 0)
            for k in range(1, chunk):
                @pl.when(n_tail == k)
                def _():
                    swiglu_rows(n_wide * chunk, k, wgs[...], wus[...], wds[...])

        for k in range(1, chunk):
            @pl.when(jnp.logical_and(n_wide == 0, n_tail == k))
            def _():
                narrow_weights_and_first(k)

    @pl.when(jnp.logical_and(n > 0, j == nj - 1))
    def _():
        for_each_subtile(n, lambda m: y_copy(g, m).start())
        last = g + 1 >= pl.num_programs(0)
        next_n = gn_ref[jnp.minimum(g + 1, pl.num_programs(0) - 1)]

        @pl.when(jnp.logical_or(last, next_n == 0))
        def _():
            for_each_subtile(n, lambda m: y_copy(g, m).wait())


def _group_ffn_call(name, layer, group_expert, group_n, row_start, tail, x, wg, wu, wd, out_rows, sub, nsub,
                    n_groups, all_full, chunk=1):
    d_ff = wg.shape[-1]
    nj = d_ff // TF_FFN

    def jmap(g, j, gn):
        return jnp.where(gn[g] > 0, j, nj - 1)

    return pl.pallas_call(
        functools.partial(_group_ffn_kernel, sub=sub, nsub=nsub, nj=nj, all_full=all_full, chunk=chunk),
        grid_spec=pltpu.PrefetchScalarGridSpec(
            num_scalar_prefetch=4,
            grid=(n_groups, nj),
            in_specs=[
                pl.BlockSpec(memory_space=pl.ANY),
                pl.BlockSpec((None, None, D, TF_FFN), lambda g, j, ge, gn, rs, tl: (layer, ge[g], 0, jmap(g, j, gn))),
                pl.BlockSpec((None, None, D, TF_FFN), lambda g, j, ge, gn, rs, tl: (layer, ge[g], 0, jmap(g, j, gn))),
                pl.BlockSpec((None, None, TF_FFN, D), lambda g, j, ge, gn, rs, tl: (layer, ge[g], jmap(g, j, gn), 0)),
            ],
            out_specs=pl.BlockSpec(memory_space=pl.ANY),
            scratch_shapes=[
                pltpu.VMEM((nsub, sub, D), bf16),
                pltpu.VMEM((nsub, sub, D), f32),
                pltpu.VMEM((2, sub, D), f32),
                pltpu.VMEM((D, TF_FFN), bf16),
                pltpu.VMEM((D, TF_FFN), bf16),
                pltpu.VMEM((TF_FFN, D), bf16),
                pltpu.SemaphoreType.DMA((2, nsub)),
            ],
        ),
        out_shape=jax.ShapeDtypeStruct((out_rows, D), f32),
        compiler_params=_cparams(("arbitrary", "arbitrary")),
        name=name,
    )(group_expert, group_n, row_start, tail, x, wg, wu, wd)


def _dense_ffn_call(l, h2, wg, wu, wd):
    n_groups = N_TOK // (SUB_DENSE * NSUB_DENSE)
    zeros = jnp.zeros((n_groups,), i32)
    return _group_ffn_call(
        "ffn_dense", l // 2, zeros, zeros + NSUB_DENSE, jnp.arange(n_groups * NSUB_DENSE, dtype=i32) * SUB_DENSE,
        jnp.zeros((2,), i32), h2, wg[:, None], wu[:, None], wd[:, None], N_TOK, SUB_DENSE, NSUB_DENSE, n_groups,
        True)


def _residual_out(i, tm, x2, ep, outs, final):
    if final:
        (g_ref,), (yp_ref, ys_ref) = ep, outs
        y = _rms(x2, g_ref[...])
        npt = N_PROMPT // tm

        @pl.when(i < npt)
        def _():
            yp_ref[...] = y

        @pl.when(i >= npt)
        def _():
            ys_ref[...] = y
    else:
        (shp, shs, scp, scs, g_ref), (x_ref, h_ref) = ep, outs
        x_ref[...] = x2
        h_ref[...] = _modulated_norm(i, tm, x2, g_ref, shp, shs, scp, scs)


def _residual_specs(l, tm, nargs, g_mix, g_final):
    if nargs == 1:
        tile = lambda i: (i, 0)
        const3 = lambda i: (l + 1, 0, 0)
        const2 = lambda i: (0, 0)
        pmap = lambda i: (jnp.minimum(i, N_PROMPT // tm - 1), 0)
        smap = lambda i: (jnp.maximum(i - N_PROMPT // tm, 0), 0)
    else:
        tile = lambda i, s: (i, 0)
        const3 = lambda i, s: (l + 1, 0, 0)
        const2 = lambda i, s: (0, 0)
        pmap = lambda i, s: (jnp.minimum(i, N_PROMPT // tm - 1), 0)
        smap = lambda i, s: (jnp.maximum(i - N_PROMPT // tm, 0), 0)
    if l == DEPTH - 1:
        return ([g_final.reshape(1, D)], [pl.BlockSpec((1, D), const2)],
                [pl.BlockSpec((tm, D), pmap), pl.BlockSpec((tm, D), smap)],
                [jax.ShapeDtypeStruct((N_PROMPT, D), f32), jax.ShapeDtypeStruct((N_SAMPLE, D), f32)])
    mod_nargs = 1 if nargs == 1 else 3
    return (None, [*_mod_specs(l + 1, 0, tm, mod_nargs), *_mod_specs(l + 1, 1, tm, mod_nargs),
                   pl.BlockSpec((None, 1, D), const3)],
            [pl.BlockSpec((tm, D), tile), pl.BlockSpec((tm, D), tile)],
            [jax.ShapeDtypeStruct((N_TOK, D), f32), jax.ShapeDtypeStruct((N_TOK, D), bf16)])


def _dense_residual_kernel(y_ref, x1_ref, gtp, gts, *rest, final):
    i = pl.program_id(0)
    n_ep = 1 if final else 5
    x2 = x1_ref[...] + _sel_mod(i, TM, gtp, gts) * y_ref[...]
    _residual_out(i, TM, x2, rest[:n_ep], rest[n_ep:], final)


def _dense_residual_call(l, y, x1, modp, mods, g_mix, g_final):
    final = l == DEPTH - 1
    gtp, gts = _mod_specs(l, 5, TM, 1)
    ep_ops, ep_specs, out_specs, out_shape = _residual_specs(l, TM, 1, g_mix, g_final)
    if ep_ops is None:
        ep_ops = [modp, mods, modp, mods, g_mix.reshape(DEPTH, 1, D)]
    return pl.pallas_call(
        functools.partial(_dense_residual_kernel, final=final),
        grid=(N_TOK // TM,),
        in_specs=[pl.BlockSpec((TM, D), lambda i: (i, 0)), pl.BlockSpec((TM, D), lambda i: (i, 0)), gtp, gts, *ep_specs],
        out_specs=out_specs,
        out_shape=out_shape,
        compiler_params=_cparams(("arbitrary",)),
        name="ffn_residual",
    )(y, x1, modp, mods, *ep_ops)


def _router_kernel(h_ref, wr_ref, route_ref, cnt_ref, carry):
    i = pl.program_id(0)

    @pl.when(i == 0)
    def _():
        carry[...] = jnp.zeros((8, 128), f32)

    logits = jnp.dot(h_ref[...].astype(bf16), wr_ref[...].astype(bf16), preferred_element_type=f32)
    lane = lax.broadcasted_iota(i32, (TM, 128), 1)
    neg = jnp.float32(-jnp.inf)
    lg = jnp.where(lane < N_EXPERTS, logits, neg)
    m1 = jnp.max(lg, axis=-1, keepdims=True)
    i1 = jnp.min(jnp.where(lg == m1, lane, 128), axis=-1, keepdims=True)
    lg2 = jnp.where(lane == i1, neg, lg)
    m2 = jnp.max(lg2, axis=-1, keepdims=True)
    i2 = jnp.min(jnp.where(lg2 == m2, lane, 128), axis=-1, keepdims=True)
    e2 = jnp.exp(m2 - m1)
    g1 = 1.0 / (1.0 + e2)
    g2 = e2 / (1.0 + e2)

    hit1 = lane == i1
    hit2 = lane == i2
    cnt = (hit1 | hit2).astype(f32)
    before = (lax.broadcasted_iota(i32, (TM, TM), 0) > lax.broadcasted_iota(i32, (TM, TM), 1))
    prefix = jnp.dot(before.astype(bf16), cnt.astype(bf16), preferred_element_type=f32) + carry[0:1, :]
    r1 = jnp.sum(jnp.where(hit1, prefix, 0.0), axis=-1, keepdims=True)
    r2 = jnp.sum(jnp.where(hit2, prefix, 0.0), axis=-1, keepdims=True)
    total = carry[...] + jnp.sum(cnt, axis=0, keepdims=True)
    carry[...] = total
    cnt_ref[...] = total

    out = jnp.where(lane == 0, i1.astype(f32), 0.0)
    out = jnp.where(lane == 1, i2.astype(f32), out)
    out = jnp.where(lane == 2, g1, out)
    out = jnp.where(lane == 3, g2, out)
    out = jnp.where(lane == 4, r1, out)
    out = jnp.where(lane == 5, r2, out)
    route_ref[...] = out


def _router_call(h2, wr_pad):
    return pl.pallas_call(
        _router_kernel,
        grid=(N_TOK // TM,),
        in_specs=[
            pl.BlockSpec((TM, D), lambda i: (i, 0)),
            pl.BlockSpec((D, 128), lambda i: (0, 0)),
        ],
        out_specs=[
            pl.BlockSpec((TM, 128), lambda i: (i, 0)),
            pl.BlockSpec((8, 128), lambda i: (0, 0)),
        ],
        out_shape=[
            jax.ShapeDtypeStruct((N_TOK, 128), f32),
            jax.ShapeDtypeStruct((8, 128), f32),
        ],
        scratch_shapes=[pltpu.VMEM((8, 128), f32)],
        compiler_params=_cparams(("arbitrary",)),
        name="moe_router",
    )(h2, wr_pad)


def _dispatch_kernel(pos_ref, pad_ref, h_ref, xs_ref, zbuf, sem, zsem, tsem):
    i = pl.program_id(0)

    def zero_copy(r):
        return pltpu.make_async_copy(zbuf.at[pl.ds(0, 1), :], xs_ref.at[pl.ds(r, 1), :], zsem)

    def tail_copy(c):
        r0 = pl.multiple_of(pad_ref[2 * N_EXPERTS] + c * SUB_MOE, 8)
        return pltpu.make_async_copy(zbuf, xs_ref.at[pl.ds(r0, SUB_MOE), :], tsem)

    def for_each_pad_row(fn):
        for e in range(N_EXPERTS):
            def body(r, carry):
                fn(r)
                return carry
            lax.fori_loop(pad_ref[e], pad_ref[N_EXPERTS + e], body, 0)

    def for_each_tail_tile(fn):
        def body(c, carry):
            fn(c)
            return carry
        lax.fori_loop(0, pad_ref[2 * N_EXPERTS + 1], body, 0)

    @pl.when(i == 0)
    def _():
        zbuf[...] = jnp.zeros((SUB_MOE, D), f32)
        for_each_pad_row(lambda r: zero_copy(r).start())
        for_each_tail_tile(lambda c: tail_copy(c).start())

    def row_copy(r, slot):
        p = pos_ref[2 * (i * TM + r) + slot]
        return pltpu.make_async_copy(h_ref.at[pl.ds(r, 1), :], xs_ref.at[pl.ds(p, 1), :], sem)

    def issue(r, carry):
        row_copy(r, 0).start(priority=0)
        row_copy(r, 1).start(priority=1)
        return carry

    lax.fori_loop(0, TM, issue, 0, unroll=8)
    for _ in range(2):
        pltpu.make_async_copy(h_ref, xs_ref.at[pl.ds(0, TM), :], sem).wait()

    @pl.when(i == 0)
    def _():
        for_each_pad_row(lambda r: zero_copy(r).wait())
        for_each_tail_tile(lambda c: tail_copy(c).wait())


def _dispatch_call(pos_flat, pad_bounds, h2):
    return pl.pallas_call(
        _dispatch_kernel,
        grid_spec=pltpu.PrefetchScalarGridSpec(
            num_scalar_prefetch=2,
            grid=(N_TOK // TM,),
            in_specs=[pl.BlockSpec((TM, D), lambda i, pos, pad: (i, 0))],
            out_specs=pl.BlockSpec(memory_space=pl.ANY),
            scratch_shapes=[pltpu.VMEM((SUB_MOE, D), f32), pltpu.SemaphoreType.DMA(()),
                            pltpu.SemaphoreType.DMA(()), pltpu.SemaphoreType.DMA(())],
        ),
        out_shape=jax.ShapeDtypeStruct((P_ROWS, D), f32),
        compiler_params=_cparams(("arbitrary",)),
        name="moe_dispatch",
    )(pos_flat, pad_bounds, h2)


def _combine_kernel(pos_ref, y_ref, x1_ref, route_ref, gtp, gts, *rest, final):
    n_ep = 1 if final else 5
    ep, outs, (buf, sem) = rest[:n_ep], rest[n_ep:-2], rest[-2:]
    i = pl.program_id(0)
    nt = pl.num_programs(0)
    slot = i % 2

    def start_tile(t, s):
        def issue(r, carry):
            for k in range(2):
                p = pos_ref[2 * (t * TM_COMB + r) + k]
                pltpu.make_async_copy(y_ref.at[pl.ds(p, 1), :], buf.at[s, k, pl.ds(r, 1), :],
                                      sem.at[s]).start(priority=k)
            return carry
        lax.fori_loop(0, TM_COMB, issue, 0, unroll=8)

    @pl.when(i == 0)
    def _():
        start_tile(0, 0)

    @pl.when(i + 1 < nt)
    def _():
        start_tile(i + 1, 1 - slot)

    for k in range(2):
        pltpu.make_async_copy(y_ref.at[pl.ds(0, TM_COMB), :], buf.at[slot, k], sem.at[slot]).wait()
    f = route_ref[:, 2:3] * buf[slot, 0] + route_ref[:, 3:4] * buf[slot, 1]
    x2 = x1_ref[...] + _sel_mod(i, TM_COMB, gtp, gts) * f
    _residual_out(i, TM_COMB, x2, ep, outs, final)


def _combine_call(l, pos_flat, y, x1, route, modp, mods, g_mix, g_final):
    final = l == DEPTH - 1
    gtp, gts = _mod_specs(l, 5, TM_COMB, 3)
    ep_ops, ep_specs, out_specs, out_shape = _residual_specs(l, TM_COMB, 2, g_mix, g_final)
    if ep_ops is None:
        ep_ops = [modp, mods, modp, mods, g_mix.reshape(DEPTH, 1, D)]
    return pl.pallas_call(
        functools.partial(_combine_kernel, final=final),
        grid_spec=pltpu.PrefetchScalarGridSpec(
            num_scalar_prefetch=1,
            grid=(N_TOK // TM_COMB,),
            in_specs=[
                pl.BlockSpec(memory_space=pl.ANY),
                pl.BlockSpec((TM_COMB, D), lambda i, pos: (i, 0)),
                pl.BlockSpec((TM_COMB, 128), lambda i, pos: (i, 0)),
                gtp, gts, *ep_specs,
            ],
            out_specs=out_specs,
            scratch_shapes=[
                pltpu.VMEM((2, 2, TM_COMB, D), f32),
                pltpu.SemaphoreType.DMA((2,)),
            ],
        ),
        out_shape=out_shape,
        compiler_params=_cparams(("arbitrary",)),
        name="moe_combine",
    )(pos_flat, y, x1, route, modp, mods, *ep_ops)


def _moe_layer(l, h2, x1, w_router, wg, wu, wd, modp, mods, g_mix, g_final):
    lm = l // 2
    wr_pad = jnp.pad(w_router[lm], ((0, 0), (0, 128 - N_EXPERTS)))
    route, cnt = _router_call(h2, wr_pad)
    expert = route[:, 0:2].astype(i32)
    rank = route[:, 4:6].astype(i32)
    counts = cnt[0, :N_EXPERTS].astype(i32)
    tiles = (counts + SUB_MOE - 1) // SUB_MOE
    tile_start = jnp.cumsum(tiles) - tiles
    row_start = tile_start * SUB_MOE
    pos_flat = (row_start[expert] + rank).reshape(-1)
    n_tiles = jnp.sum(tiles)
    tail = jnp.stack([n_tiles * SUB_MOE, MAX_TILES - n_tiles])
    pad_bounds = jnp.concatenate([row_start + counts, row_start + tiles * SUB_MOE, tail])
    groups = (tiles + NSUB_MOE - 1) // NSUB_MOE
    group_end = jnp.cumsum(groups)
    n_groups = group_end[-1]
    gid = jnp.arange(MAX_GROUPS, dtype=i32)
    gcl = jnp.minimum(gid, n_groups - 1)
    group_expert = jnp.sum((gcl[:, None] >= group_end[None, :]).astype(i32), axis=1)
    per_group = tiles // jnp.maximum(groups, 1)
    extra = tiles - per_group * groups
    k = gcl - (group_end - groups)[group_expert]
    first = k * per_group[group_expert] + jnp.minimum(k, extra[group_expert])
    group_n = jnp.where(gid < n_groups, per_group[group_expert] + (k < extra[group_expert]).astype(i32), 0)
    sub_start = (tile_start[group_expert] + first)[:, None] + jnp.arange(NSUB_MOE, dtype=i32)[None, :]
    xs = _dispatch_call(pos_flat, pad_bounds, h2)
    y = _group_ffn_call("moe_experts", lm, group_expert, group_n, (sub_start * SUB_MOE).reshape(-1), tail, xs,
                        wg, wu, wd, P_ROWS, SUB_MOE, NSUB_MOE, MAX_GROUPS, False, CHUNK_MOE)
    return _combine_call(l, pos_flat, y, x1, route, modp, mods, g_mix, g_final)


def kernel(x_prompt, x_sample, state_pool, state_conv, c_prompt, c_sample, w_ada, b_ada, g_mix, g_ffn, w_in, w_pool, pool_scale, w_dw, b_dw, conv_ln_g, conv_ln_b, sgu_ln_g, sgu_ln_b, w_spatial, b_spatial, w_out, w_ff_gate, w_ff_up, w_ff_down, w_router, w_exp_gate, w_exp_up, w_exp_down, g_final):
    c_all = jnp.concatenate([c_sample, c_prompt, jnp.zeros((N_SEQ_PAD - N_SAMPLE - BATCH, D), f32)], axis=0)
    modp = mods = _ada_call(c_all, w_ada, b_ada)
    hp_t = state_pool.transpose(0, 2, 1, 3)
    hc_t = state_conv.transpose(0, 2, 1, 3)
    bs_bcast = jnp.broadcast_to(b_spatial[:, :, :, None], (DEPTH, N_HEADS, CHUNK, CHUNK))
    ws_small = w_spatial[:, :, :DEC_SEQ, :DEC_SEQ].reshape(DEPTH, -1)
    bs_small = b_spatial[:, :, :DEC_SEQ].reshape(DEPTH, -1)
    w_in_bf = w_in.astype(bf16)

    x, h1 = _prenorm_call(x_prompt, x_sample, modp, mods, g_mix)
    pool_p, conv_p, pool_s, conv_s, chunk_v = [], [], [], [], []
    for l in range(DEPTH):
        proj = _inproj_call(l, h1, w_in_bf)
        proj_s = proj[N_PROMPT:].reshape(DEC_BATCH, DEC_SEQ, P_IN)
        mix_s, glu_s, v_s = _mixs_call(l, ws_small[l], bs_small[l], proj_s.transpose(1, 0, 2), hp_t, hc_t,
                                       w_pool, pool_scale, w_dw, b_dw, conv_ln_g, conv_ln_b, sgu_ln_g, sgu_ln_b)
        mixcat, npool, nconv = _mixp_call(l, proj, mix_s.transpose(1, 0, 2).reshape(N_SAMPLE, D), w_pool, pool_scale,
                                          w_dw, b_dw, conv_ln_g, conv_ln_b, sgu_ln_g, sgu_ln_b, w_spatial, bs_bcast)
        pool_p.append(npool[:, POOL_HALO - POOL_HIST:])
        conv_p.append(nconv[:, CONV_HALO - CONV_HIST:])
        pool_s.append(jnp.concatenate([state_pool[l][:, DEC_SEQ:], proj_s[:, :, :D_POOL]], axis=1))
        conv_s.append(jnp.concatenate([state_conv[l][:, DEC_SEQ:], glu_s.transpose(1, 0, 2)], axis=1))
        chunk_v.append(v_s.transpose(1, 0, 2))
        x1, h2 = _outproj_call(l, mixcat, w_out, x, modp, mods, g_ffn, f32)
        if l % 2 == 0:
            y = _dense_ffn_call(l, h2, w_ff_gate, w_ff_up, w_ff_down)
            out = _dense_residual_call(l, y, x1, modp, mods, g_mix, g_final)
        else:
            out = _moe_layer(l, h2, x1, w_router, w_exp_gate, w_exp_up, w_exp_down, modp, mods, g_mix, g_final)
        if l < DEPTH - 1:
            x, h1 = out
    y_prompt, y_sample = out
    return (y_prompt.reshape(BATCH, SEQ, D), y_sample.reshape(DEC_BATCH, DEC_SEQ, D),
            jnp.stack(pool_p), jnp.stack(conv_p), jnp.stack(pool_s), jnp.stack(conv_s), jnp.stack(chunk_v))
```
